```python
import functools
import jax, jax.numpy as jnp
from jax import lax
import numpy as np

D_MODEL = 4096
BATCH = 8
SEQ = 2048
DEPTH = 1
DEC_BATCH = 8
DEC_SEQ = 32
PAST_LEN = 1024

CHUNK = 64
N_META = 16
MIX_WIDTH = D_MODEL
ATTN_WIDTH = MIX_WIDTH // 2
N_HEADS = 16
HEAD_DIM = ATTN_WIDTH // N_HEADS
CONV_DIM = MIX_WIDTH - ATTN_WIDTH
CONV_GROUPS = 16
CONV_WIDTH = 31
IN_COLS = 2 * CONV_DIM + 3 * ATTN_WIDTH + N_HEADS
Q_BLOCK = 128
N_GROUPS = 8
EXPERTS_PER_GROUP = 8
N_EXPERTS = N_GROUPS * EXPERTS_PER_GROUP
TOP_K = 2
EXPERT_FF = D_MODEL // 4
EXPERT_BLOCK = 128
EPS = 1e-6
NEG_INF = -1e30

kernel_name = 'hybrid_fox_conformer_hmoe_stream_step'


def rms_norm(x, g):
    xf = x.astype(jnp.float32)
    y = xf * lax.rsqrt(jnp.mean(xf * xf, axis=-1, keepdims=True) + EPS)
    return (y * g.astype(jnp.float32)).astype(x.dtype)


def project(h, norm_g, w_in, b_forget, q_norm_g, k_norm_g):
    b, t, _ = h.shape
    z = rms_norm(h, norm_g) @ w_in
    c0 = CONV_DIM
    c1 = 2 * CONV_DIM
    c2 = c1 + ATTN_WIDTH
    c3 = c2 + ATTN_WIDTH
    c4 = c3 + ATTN_WIDTH
    glu = z[..., :c0] * jax.nn.sigmoid(z[..., c0:c1])
    q = rms_norm(z[..., c1:c2].reshape(b, t, N_HEADS, HEAD_DIM), q_norm_g) * (HEAD_DIM ** -0.5)
    k = rms_norm(z[..., c2:c3].reshape(b, t, N_HEADS, HEAD_DIM), k_norm_g)
    v = z[..., c3:c4].reshape(b, t, N_HEADS, HEAD_DIM)
    logf = jax.nn.log_sigmoid(z[..., c4:].astype(jnp.float32) + b_forget.astype(jnp.float32))
    return glu, q, k, v, logf


def fox_attend(q, cq, qpos, k, v, ck, kpos):
    s = jnp.einsum('bqhd,bkhd->bhqk', q, k, preferred_element_type=jnp.float32)
    bias = jnp.swapaxes(cq, 1, 2)[:, :, :, None] - jnp.swapaxes(ck, 1, 2)[:, :, None, :]
    mask = (kpos[None, :] <= qpos[:, None])[None, None]
    p = jax.nn.softmax(jnp.where(mask, s + bias, NEG_INF), axis=-1).astype(v.dtype)
    return jnp.einsum('bhqk,bkhd->bqhd', p, v)


def fox_prompt(q, k, v, logf):
    b, t = q.shape[0], q.shape[1]
    cum = jnp.cumsum(logf, axis=1)
    n_blk = -(-t // Q_BLOCK)
    pad = n_blk * Q_BLOCK - t
    padt = lambda a: jnp.pad(a, [(0, 0), (0, pad)] + [(0, 0)] * (a.ndim - 2))
    qp, kp, vp, cp = padt(q), padt(k), padt(v), padt(cum)
    pos = jnp.arange(n_blk * Q_BLOCK)
    q_blocks = qp.reshape(b, n_blk, Q_BLOCK, N_HEADS, HEAD_DIM).swapaxes(0, 1)
    c_blocks = cp.reshape(b, n_blk, Q_BLOCK, N_HEADS).swapaxes(0, 1)
    p_blocks = pos.reshape(n_blk, Q_BLOCK)
    out = lax.map(lambda a: fox_attend(a[0], a[1], a[2], kp, vp, cp, pos), (q_blocks, c_blocks, p_blocks))
    return out.swapaxes(0, 1).reshape(b, n_blk * Q_BLOCK, N_HEADS, HEAD_DIM)[:, :t]


def fox_sample(cache_k, cache_v, cache_logf, q, k, v, logf):
    past, t = cache_k.shape[1], q.shape[1]
    k_all = jnp.concatenate([cache_k.astype(k.dtype), k], axis=1)
    v_all = jnp.concatenate([cache_v.astype(v.dtype), v], axis=1)
    cum = jnp.cumsum(jnp.concatenate([cache_logf.astype(jnp.float32), logf], axis=1), axis=1)
    kpos = jnp.arange(past + t)
    qpos = past + jnp.arange(t)
    return fox_attend(q, cum[:, past:], qpos, k_all, v_all, cum, kpos)


def conv_module(glu, hist, conv_w, conv_b, ln_g, ln_b):
    b, t, _ = glu.shape
    full = jnp.concatenate([hist.astype(glu.dtype), glu], axis=1)
    y = lax.conv_general_dilated(full, conv_w.astype(glu.dtype)[:, None, :], window_strides=(1,),
                                 padding='VALID', dimension_numbers=('NWC', 'WIO', 'NWC'),
                                 feature_group_count=CONV_DIM)
    y = y.astype(jnp.float32) + conv_b.astype(jnp.float32)
    yg = y.reshape(b, t, CONV_GROUPS, CONV_DIM // CONV_GROUPS)
    mu = jnp.mean(yg, axis=-1, keepdims=True)
    var = jnp.mean(jnp.square(yg - mu), axis=-1, keepdims=True)
    yn = ((yg - mu) * lax.rsqrt(var + EPS)).reshape(b, t, CONV_DIM) * ln_g.astype(jnp.float32) + ln_b.astype(jnp.float32)
    return jax.nn.silu(yn).astype(glu.dtype), full[:, -(CONV_WIDTH - 1):]


def expert_dispatch(x2, e_id, gate, w_gate, w_up, w_down):
    n, d = x2.shape
    m = n * TOP_K
    e_flat = e_id.reshape(m)
    order = jnp.argsort(e_flat)
    sorted_e = e_flat[order]
    sorted_tok = order // TOP_K
    counts = jnp.zeros((N_EXPERTS,), jnp.int32).at[e_flat].add(1)
    start = jnp.cumsum(counts) - counts
    padded = (counts + EXPERT_BLOCK - 1) // EXPERT_BLOCK * EXPERT_BLOCK
    pad_end = jnp.cumsum(padded)
    pad_start = pad_end - padded
    dest = pad_start[sorted_e] + (jnp.arange(m) - start[sorted_e])
    n_blk = -(-m // EXPERT_BLOCK) + N_EXPERTS
    src = jnp.full((n_blk * EXPERT_BLOCK,), n, jnp.int32).at[dest].set(sorted_tok)
    x_pad = jnp.concatenate([x2, jnp.zeros((1, d), x2.dtype)], axis=0)[src]
    blk_e = jnp.minimum(jnp.searchsorted(pad_end, jnp.arange(n_blk) * EXPERT_BLOCK, side='right'), N_EXPERTS - 1)

    def run(args):
        xb, e = args
        hid = jax.nn.silu(xb @ w_gate[e]) * (xb @ w_up[e])
        return hid @ w_down[e]

    y_pad = lax.map(run, (x_pad.reshape(n_blk, EXPERT_BLOCK, d), blk_e)).reshape(n_blk * EXPERT_BLOCK, d)
    y_slot = y_pad[dest] * gate.reshape(m)[order][:, None].astype(x2.dtype)
    return jax.ops.segment_sum(y_slot, sorted_tok, num_segments=n)


def hier_moe(x2, w_rg, b_rg, w_re, b_re, w_gate, w_up, w_down):
    n = x2.shape[0]
    lg = (x2 @ w_rg).astype(jnp.float32) + b_rg.astype(jnp.float32)
    _, g = lax.top_k(lg, 1)
    p_g = jnp.take_along_axis(jax.nn.softmax(lg, axis=-1), g, axis=-1)
    le = ((x2 @ w_re).astype(jnp.float32) + b_re.astype(jnp.float32)).reshape(n, N_GROUPS, EXPERTS_PER_GROUP)
    le_g = jnp.take_along_axis(le, g[:, :, None], axis=1)[:, 0]
    top_l, top_j = lax.top_k(le_g, TOP_K)
    gate = jax.nn.softmax(top_l, axis=-1) * p_g
    e_id = g * EXPERTS_PER_GROUP + top_j
    return expert_dispatch(x2, e_id, gate, w_gate, w_up, w_down)


def layer(h, conv_hist, attn_fn, norm_mix_g, w_in, b_forget, q_norm_g, k_norm_g, conv_w, conv_b,
          conv_ln_g, conv_ln_b, w_out, norm_ffn_g, w_router_group, b_router_group,
          w_router_expert, b_router_expert, w_gate, w_up, w_down):
    b, t, d = h.shape
    glu, q, k, v, logf = project(h, norm_mix_g, w_in, b_forget, q_norm_g, k_norm_g)
    conv_out, conv_state = conv_module(glu, conv_hist, conv_w, conv_b, conv_ln_g, conv_ln_b)
    attn = attn_fn(q, k, v, logf)
    h = h + jnp.concatenate([attn.reshape(b, t, ATTN_WIDTH), conv_out], axis=-1) @ w_out
    ffn = hier_moe(rms_norm(h, norm_ffn_g).reshape(b * t, d), w_router_group, b_router_group,
                   w_router_expert, b_router_expert, w_gate, w_up, w_down)
    return h + ffn.reshape(b, t, d), (k, v, logf, conv_state)


def setup_inputs(seed: int = 0) -> dict:
    key = jax.random.key(seed)
    ks = jax.random.split(key, 26)
    nrm = lambda k, shape, scale: jax.random.normal(k, shape, jnp.float32) * scale
    return {
        'x_prompt': nrm(ks[0], (BATCH, SEQ, D_MODEL), 1.0),
        'x_sample': nrm(ks[1], (DEC_BATCH, DEC_SEQ, D_MODEL), 1.0),
        'cache_k': nrm(ks[2], (DEPTH, DEC_BATCH, PAST_LEN, N_HEADS, HEAD_DIM), 1.0),
        'cache_v': nrm(ks[3], (DEPTH, DEC_BATCH, PAST_LEN, N_HEADS, HEAD_DIM), 1.0),
        'cache_logf': jax.nn.log_sigmoid(4.0 + nrm(ks[4], (DEPTH, DEC_BATCH, PAST_LEN, N_HEADS), 1.0)),
        'cache_conv': nrm(ks[5], (DEPTH, DEC_BATCH, CONV_WIDTH - 1, CONV_DIM), 0.5),
        'meta_tokens': nrm(ks[6], (N_META, D_MODEL), 1.0),
        'norm_mix_g': 1.0 + nrm(ks[7], (DEPTH, D_MODEL), 0.01),
        'w_in': nrm(ks[8], (DEPTH, D_MODEL, IN_COLS), D_MODEL ** -0.5),
        'b_forget': jax.random.uniform(ks[9], (DEPTH, N_HEADS), jnp.float32, 2.0, 6.0),
        'q_norm_g': 1.0 + nrm(ks[10], (DEPTH, HEAD_DIM), 0.01),
        'k_norm_g': 1.0 + nrm(ks[11], (DEPTH, HEAD_DIM), 0.01),
        'conv_w': nrm(ks[12], (DEPTH, CONV_WIDTH, CONV_DIM), CONV_WIDTH ** -0.5),
        'conv_b': nrm(ks[13], (DEPTH, CONV_DIM), 0.01),
        'conv_ln_g': 1.0 + nrm(ks[14], (DEPTH, CONV_DIM), 0.01),
        'conv_ln_b': nrm(ks[15], (DEPTH, CONV_DIM), 0.01),
        'w_out': nrm(ks[16], (DEPTH, MIX_WIDTH, D_MODEL), MIX_WIDTH ** -0.5),
        'norm_ffn_g': 1.0 + nrm(ks[17], (DEPTH, D_MODEL), 0.01),
        'w_router_group': nrm(ks[18], (DEPTH, D_MODEL, N_GROUPS), D_MODEL ** -0.5),
        'b_router_group': nrm(ks[19], (DEPTH, N_GROUPS), 0.01),
        'w_router_expert': nrm(ks[20], (DEPTH, D_MODEL, N_EXPERTS), D_MODEL ** -0.5),
        'b_router_expert': nrm(ks[21], (DEPTH, N_EXPERTS), 0.01),
        'w_gate': nrm(ks[22], (DEPTH, N_EXPERTS, D_MODEL, EXPERT_FF), D_MODEL ** -0.5),
        'w_up': nrm(ks[23], (DEPTH, N_EXPERTS, D_MODEL, EXPERT_FF), D_MODEL ** -0.5),
        'w_down': nrm(ks[24], (DEPTH, N_EXPERTS, EXPERT_FF, D_MODEL), EXPERT_FF ** -0.5),
    }


def reference(x_prompt, x_sample, cache_k, cache_v, cache_logf, cache_conv, meta_tokens, norm_mix_g,
              w_in, b_forget, q_norm_g, k_norm_g, conv_w, conv_b, conv_ln_g, conv_ln_b, w_out,
              norm_ffn_g, w_router_group, b_router_group, w_router_expert, b_router_expert,
              w_gate, w_up, w_down):
    bp = x_prompt.shape[0]
    h_p = jnp.concatenate([jnp.broadcast_to(meta_tokens.astype(x_prompt.dtype)[None], (bp, N_META, D_MODEL)), x_prompt], axis=1)
    h_s = x_sample
    kp, vp, lp, cp, ks_, vs_, ls_, cs_ = [], [], [], [], [], [], [], []
    for l in range(DEPTH):
        lw = (norm_mix_g[l], w_in[l], b_forget[l], q_norm_g[l], k_norm_g[l], conv_w[l], conv_b[l],
              conv_ln_g[l], conv_ln_b[l], w_out[l], norm_ffn_g[l], w_router_group[l], b_router_group[l],
              w_router_expert[l], b_router_expert[l], w_gate[l], w_up[l], w_down[l])
        zero_hist = jnp.zeros((bp, CONV_WIDTH - 1, CONV_DIM), h_p.dtype)
        h_p, st_p = layer(h_p, zero_hist, fox_prompt, *lw)
        attn_s = functools.partial(fox_sample, cache_k[l], cache_v[l], cache_logf[l])
        h_s, st_s = layer(h_s, cache_conv[l], attn_s, *lw)
        kp.append(st_p[0]); vp.append(st_p[1]); lp.append(st_p[2]); cp.append(st_p[3])
        ks_.append(st_s[0]); vs_.append(st_s[1]); ls_.append(st_s[2]); cs_.append(st_s[3])
    return (h_p[:, N_META:], h_s, jnp.stack(kp), jnp.stack(vp), jnp.stack(lp), jnp.stack(cp),
            jnp.stack(ks_), jnp.stack(vs_), jnp.stack(ls_), jnp.stack(cs_))
```

```python
import functools

import jax
import jax.numpy as jnp
from jax import lax
from jax.experimental import pallas as pl
from jax.experimental.pallas import tpu as pltpu

F32 = jnp.float32
BF16 = jnp.bfloat16

N_META = 16
CONV_GROUPS = 16
TOP_K = 2
EPS = 1e-6
NEG_INF = -1e30

LANES = 128
ATTN_BLOCK = 256
MOE_SUB = 256
MOE_SUBS_PER_CHUNK = 3
VMEM_LIMIT = 56 * 1024 * 1024


def _cdiv(a, b):
    return -(-a // b)


def _div_tile(n, target, mult):
    best = None
    for d in range(mult, min(n, target) + 1, mult):
        if n % d == 0:
            best = d
    assert best is not None, (n, target, mult)
    return best


def _params(*sem):
    return pltpu.CompilerParams(dimension_semantics=sem, vmem_limit_bytes=VMEM_LIMIT)


def _rmsnorm_kernel(x_ref, g_ref, o_ref):
    x = x_ref[...]
    ms = jnp.mean(x * x, axis=-1, keepdims=True)
    o_ref[...] = (x * lax.rsqrt(ms + EPS) * g_ref[...]).astype(o_ref.dtype)


def _rmsnorm(x, g, tm):
    n, d = x.shape
    return pl.pallas_call(
        _rmsnorm_kernel,
        out_shape=jax.ShapeDtypeStruct((n, d), BF16),
        grid=(n // tm,),
        in_specs=[pl.BlockSpec((tm, d), lambda i: (i, 0)), pl.BlockSpec((1, d), lambda i: (0, 0))],
        out_specs=pl.BlockSpec((tm, d), lambda i: (i, 0)),
        compiler_params=_params("parallel"),
        name="rmsnorm",
    )(x, g.reshape(1, d))


def _glu_kernel(x_ref, wa_ref, wb_ref, o_ref):
    x = x_ref[...]
    a = jnp.dot(x, wa_ref[...], preferred_element_type=F32)
    b = jnp.dot(x, wb_ref[...], preferred_element_type=F32)
    o_ref[...] = a * jax.nn.sigmoid(b)


def _proj_glu(xn, w, conv_dim, tm, tn):
    n, d = xn.shape
    nb = conv_dim // tn
    return pl.pallas_call(
        _glu_kernel,
        out_shape=jax.ShapeDtypeStruct((n, conv_dim), F32),
        grid=(n // tm, nb),
        in_specs=[pl.BlockSpec((tm, d), lambda i, j: (i, 0)),
                  pl.BlockSpec((d, tn), lambda i, j: (0, j)),
                  pl.BlockSpec((d, tn), lambda i, j: (0, nb + j))],
        out_specs=pl.BlockSpec((tm, tn), lambda i, j: (i, j)),
        compiler_params=_params("parallel", "arbitrary"),
        name="proj_glu",
    )(xn, w, w)


def _headnorm_kernel(x_ref, w_ref, g_ref, *o_refs, scale, hd):
    z = jnp.dot(x_ref[...], w_ref[...], preferred_element_type=F32)
    g = g_ref[...]
    for h in range(z.shape[1] // hd):
        zh = z[:, h * hd:(h + 1) * hd]
        y = zh * lax.rsqrt(jnp.mean(zh * zh, axis=-1, keepdims=True) + EPS) * g
        if scale is not None:
            y = y * scale
        for o_ref in o_refs:
            o_ref[:, h * hd:(h + 1) * hd] = y.astype(o_ref.dtype)


def _proj_headnorm(xn, w, g, col0, width, hd, scale, out_dtypes, tm, tn):
    n, d = xn.shape
    nb = width // tn
    b0 = col0 // tn
    outs = pl.pallas_call(
        functools.partial(_headnorm_kernel, scale=scale, hd=hd),
        out_shape=[jax.ShapeDtypeStruct((n, width), dt) for dt in out_dtypes],
        grid=(n // tm, nb),
        in_specs=[pl.BlockSpec((tm, d), lambda i, j: (i, 0)),
                  pl.BlockSpec((d, tn), lambda i, j: (0, b0 + j)),
                  pl.BlockSpec((1, hd), lambda i, j: (0, 0))],
        out_specs=[pl.BlockSpec((tm, tn), lambda i, j: (i, j)) for _ in out_dtypes],
        compiler_params=_params("parallel", "arbitrary"),
        name="proj_headnorm",
    )(xn, w, g.reshape(1, hd))
    return outs


def _plain_kernel(x_ref, w_ref, *o_refs):
    z = jnp.dot(x_ref[...], w_ref[...], preferred_element_type=F32)
    for o_ref in o_refs:
        o_ref[...] = z.astype(o_ref.dtype)


def _proj_plain(xn, w, col0, width, out_dtypes, tm, tn):
    n, d = xn.shape
    nb = width // tn
    b0 = col0 // tn
    return pl.pallas_call(
        _plain_kernel,
        out_shape=[jax.ShapeDtypeStruct((n, width), dt) for dt in out_dtypes],
        grid=(n // tm, nb),
        in_specs=[pl.BlockSpec((tm, d), lambda i, j: (i, 0)),
                  pl.BlockSpec((d, tn), lambda i, j: (0, b0 + j))],
        out_specs=[pl.BlockSpec((tm, tn), lambda i, j: (i, j)) for _ in out_dtypes],
        compiler_params=_params("parallel", "arbitrary"),
        name="proj_plain",
    )(xn, w)


def _logf_kernel(x_ref, w_ref, b_ref, o_ref):
    z = jnp.dot(x_ref[...], w_ref[...], preferred_element_type=F32) + b_ref[...]
    o_ref[...] = jnp.minimum(z, 0.0) - jnp.log1p(jnp.exp(-jnp.abs(z)))


def _proj_logf(xn, wf, bf, tm):
    n, d = xn.shape
    return pl.pallas_call(
        _logf_kernel,
        out_shape=jax.ShapeDtypeStruct((n, LANES), F32),
        grid=(n // tm,),
        in_specs=[pl.BlockSpec((tm, d), lambda i: (i, 0)),
                  pl.BlockSpec((d, LANES), lambda i: (0, 0)),
                  pl.BlockSpec((1, LANES), lambda i: (0, 0))],
        out_specs=pl.BlockSpec((tm, LANES), lambda i: (i, 0)),
        compiler_params=_params("parallel"),
        name="proj_logf",
    )(xn, wf, bf)


def _conv_kernel(glu_ref, hist_ref, cw_ref, cb_ref, lg_ref, lb_ref, o_ref, full_sc, *, t, tt, width):
    hp = hist_ref.shape[1]
    full_sc[0:hp, :] = hist_ref[0]
    full_sc[hp:hp + t, :] = glu_ref[0]
    cb = cb_ref[...]
    lg = lg_ref[...]
    lb = lb_ref[...]

    def chunk(ci, carry):
        t0 = pl.multiple_of(ci * tt, 16)
        win = full_sc[pl.ds(t0, tt + hp), :]
        acc = jnp.zeros((tt, LANES), F32)
        for w in range(width):
            acc = acc + win[2 + w:2 + w + tt, :] * cw_ref[w:w + 1, :]
        y = acc + cb
        mu = jnp.mean(y, axis=-1, keepdims=True)
        dlt = y - mu
        var = jnp.mean(dlt * dlt, axis=-1, keepdims=True)
        yn = dlt * lax.rsqrt(var + EPS) * lg + lb
        o_ref[0, pl.ds(t0, tt), :] = (yn * jax.nn.sigmoid(yn)).astype(o_ref.dtype)
        return carry

    lax.fori_loop(0, t // tt, chunk, 0)


def _conv_module(glu, hist, conv_w, conv_b, ln_g, ln_b):
    b, t, c = glu.shape
    width = conv_w.shape[0]
    hp = hist.shape[1]
    assert hp == 32 and width + 1 == hp and c // CONV_GROUPS == LANES
    tt = _div_tile(t, 64, 16)
    cw = jnp.concatenate([conv_w, jnp.zeros((hp - width, c), F32)], axis=0)
    vec = lambda v: v.reshape(1, c)
    return pl.pallas_call(
        functools.partial(_conv_kernel, t=t, tt=tt, width=width),
        out_shape=jax.ShapeDtypeStruct((b, t, c), BF16),
        grid=(b, c // LANES),
        in_specs=[pl.BlockSpec((1, t, LANES), lambda i, j: (i, 0, j)),
                  pl.BlockSpec((1, hp, LANES), lambda i, j: (i, 0, j)),
                  pl.BlockSpec((hp, LANES), lambda i, j: (0, j)),
                  pl.BlockSpec((1, LANES), lambda i, j: (0, j)),
                  pl.BlockSpec((1, LANES), lambda i, j: (0, j)),
                  pl.BlockSpec((1, LANES), lambda i, j: (0, j))],
        out_specs=pl.BlockSpec((1, t, LANES), lambda i, j: (i, 0, j)),
        scratch_shapes=[pltpu.VMEM((hp + t, LANES), F32)],
        compiler_params=_params("parallel", "parallel"),
        name="conv_ln_swish",
    )(glu, hist, cw, vec(conv_b), vec(ln_g), vec(ln_b))


def _cumsum_kernel(x_ref, o_ref, *, nblk, blk):
    r = lax.broadcasted_iota(jnp.int32, (blk, blk), 0)
    c = lax.broadcasted_iota(jnp.int32, (blk, blk), 1)
    tri = (c <= r).astype(F32)
    carry = jnp.zeros((1, x_ref.shape[2]), F32)
    for i in range(nblk):
        x = x_ref[0, i * blk:(i + 1) * blk, :]
        o_ref[0, i * blk:(i + 1) * blk, :] = jnp.dot(
            tri, x, precision=lax.Precision.HIGHEST, preferred_element_type=F32) + carry
        carry = carry + jnp.sum(x, axis=0, keepdims=True)


def _cumsum_time(x, blk):
    b, t, h = x.shape
    return pl.pallas_call(
        functools.partial(_cumsum_kernel, nblk=t // blk, blk=blk),
        out_shape=jax.ShapeDtypeStruct((b, t, h), F32),
        grid=(b,),
        in_specs=[pl.BlockSpec((1, t, h), lambda i: (i, 0, 0))],
        out_specs=pl.BlockSpec((1, t, h), lambda i: (i, 0, 0)),
        compiler_params=_params("parallel"),
        name="logf_cumsum",
    )(x)


def _attn_update(q, k, v, cq, ck, mask, m, l, acc):
    s = lax.dot_general(q, k, (((1,), (1,)), ((), ())), preferred_element_type=F32)
    s = s + (cq - ck)
    if mask is not None:
        s = jnp.where(mask, s, NEG_INF)
    m_new = jnp.maximum(m, jnp.max(s, axis=-1, keepdims=True))
    alpha = jnp.exp(m - m_new)
    p = jnp.exp(s - m_new)
    l_new = alpha * l + jnp.sum(p, axis=-1, keepdims=True)
    acc_new = alpha * acc + jnp.dot(p.astype(BF16), v, preferred_element_type=F32)
    return m_new, l_new, acc_new


def _softmax_init(tq, d):
    return (jnp.full((tq, 1), NEG_INF, F32), jnp.zeros((tq, 1), F32), jnp.zeros((tq, d), F32))


def _head_column(c, h):
    sel = lax.broadcasted_iota(jnp.int32, c.shape, 1) == h
    return jnp.sum(jnp.where(sel, c, 0.0), axis=-1, keepdims=True)


def _causal_mask(n):
    r = lax.broadcasted_iota(jnp.int32, (n, n), 0)
    c = lax.broadcasted_iota(jnp.int32, (n, n), 1)
    return c <= r


def _attn_prompt_kernel(q_ref, k_ref, v_ref, cum_ref, cumt_ref, o_ref, *, t, bq):
    h = pl.program_id(1)
    d = q_ref.shape[2]
    nfull = t // bq
    tail = t - nfull * bq

    def finish(rows, size, carry):
        _, l, acc = carry
        o_ref[0, pl.ds(rows, size), :] = (acc / l).astype(o_ref.dtype)

    def qblock(qi, carry0):
        qs = pl.multiple_of(qi * bq, bq)
        q = q_ref[0, pl.ds(qs, bq), :]
        cq = _head_column(cum_ref[0, pl.ds(qs, bq), :], h)

        def kv(kj, carry):
            ks = pl.multiple_of(kj * bq, bq)
            return _attn_update(q, k_ref[0, pl.ds(ks, bq), :], v_ref[0, pl.ds(ks, bq), :],
                                cq, cumt_ref[0, 0, kj], None, *carry)

        carry = lax.fori_loop(0, qi, kv, _softmax_init(bq, d))
        carry = _attn_update(q, k_ref[0, pl.ds(qs, bq), :], v_ref[0, pl.ds(qs, bq), :],
                             cq, cumt_ref[0, 0, qi], _causal_mask(bq), *carry)
        finish(qs, bq, carry)
        return carry0

    lax.fori_loop(0, nfull, qblock, 0)

    if tail:
        qs = nfull * bq
        q = q_ref[0, qs:qs + tail, :]
        cq = _head_column(cum_ref[0, qs:qs + tail, :], h)

        def kv_tail(kj, carry):
            ks = pl.multiple_of(kj * bq, bq)
            return _attn_update(q, k_ref[0, pl.ds(ks, bq), :], v_ref[0, pl.ds(ks, bq), :],
                                cq, cumt_ref[0, 0, kj], None, *carry)

        carry = lax.fori_loop(0, nfull, kv_tail, _softmax_init(tail, d))
        carry = _attn_update(q, k_ref[0, qs:qs + tail, :], v_ref[0, qs:qs + tail, :],
                             cq, cumt_ref[0, 0, nfull][:, :tail], _causal_mask(tail), *carry)
        finish(qs, tail, carry)


def _blocked_rows(cum, blk):
    b, t, h = cum.shape
    nblk = _cdiv(t, blk)
    c = jnp.pad(cum, ((0, 0), (0, nblk * blk - t), (0, 0)))
    return c.transpose(0, 2, 1).reshape(b, h, nblk, 1, blk)


def _attn_prompt(q, k, v, cum, n_heads):
    b, t, w = q.shape
    d = w // n_heads
    bq = ATTN_BLOCK
    assert (t % bq) % 16 == 0
    cumt = _blocked_rows(cum, bq)
    nblk = cumt.shape[2]
    blk = lambda: pl.BlockSpec((1, t, d), lambda i, j: (i, 0, j))
    return pl.pallas_call(
        functools.partial(_attn_prompt_kernel, t=t, bq=bq),
        out_shape=jax.ShapeDtypeStruct((b, t, w), BF16),
        grid=(b, n_heads),
        in_specs=[blk(), blk(), blk(),
                  pl.BlockSpec((1, t, n_heads), lambda i, j: (i, 0, 0)),
                  pl.BlockSpec((1, 1, nblk, 1, bq), lambda i, j: (i, j, 0, 0, 0))],
        out_specs=blk(),
        compiler_params=_params("parallel", "parallel"),
        name="fox_attn_prompt",
    )(q, k, v, cum, cumt)


def _attn_sample_kernel(q_ref, ck_ref, cv_ref, k_ref, v_ref, cum_ref, cumt_ref, o_ref, *, past, bk):
    h = pl.program_id(1)
    tq, d = q_ref.shape[1], q_ref.shape[2]
    q = q_ref[0]
    cq = _head_column(cum_ref[0], h)
    carry = _softmax_init(tq, d)
    for kj in range(past // bk):
        kb = ck_ref[0, kj * bk:(kj + 1) * bk, :].astype(BF16)
        vb = cv_ref[0, kj * bk:(kj + 1) * bk, :].astype(BF16)
        carry = _attn_update(q, kb, vb, cq, cumt_ref[0, 0, kj], None, *carry)
    carry = _attn_update(q, k_ref[0], v_ref[0], cq, cumt_ref[0, 0, past // bk][:, :tq], _causal_mask(tq), *carry)
    _, l, acc = carry
    o_ref[0] = (acc / l).astype(o_ref.dtype)


def _attn_sample(q, cache_k, cache_v, k, v, cum, n_heads):
    b, tq, w = q.shape
    d = w // n_heads
    past = cache_k.shape[1]
    bk = ATTN_BLOCK
    assert past % bk == 0 and tq <= bk
    cumt = _blocked_rows(cum, bk)
    nblk = cumt.shape[2]
    cum_q = cum[:, past:]
    new = lambda: pl.BlockSpec((1, tq, d), lambda i, j: (i, 0, j))
    old = lambda: pl.BlockSpec((1, past, d), lambda i, j: (i, 0, j))
    return pl.pallas_call(
        functools.partial(_attn_sample_kernel, past=past, bk=bk),
        out_shape=jax.ShapeDtypeStruct((b, tq, w), BF16),
        grid=(b, n_heads),
        in_specs=[new(), old(), old(), new(), new(),
                  pl.BlockSpec((1, tq, n_heads), lambda i, j: (i, 0, 0)),
                  pl.BlockSpec((1, 1, nblk, 1, bk), lambda i, j: (i, j, 0, 0, 0))],
        out_specs=new(),
        compiler_params=_params("parallel", "parallel"),
        name="fox_attn_sample",
    )(q, cache_k, cache_v, k, v, cum_q, cumt)


def _outproj_kernel(*refs, aliased):
    if aliased:
        refs = refs[1:]
    a_ref, c_ref, wa_ref, wc_ref, h_ref, o_ref = refs
    o_ref[...] = (h_ref[...]
                  + jnp.dot(a_ref[...], wa_ref[...], preferred_element_type=F32)
                  + jnp.dot(c_ref[...], wc_ref[...], preferred_element_type=F32))


def _outproj(attn, conv, w_out, h, n_total, row0, tm, tn, prev=None):
    n, wa = attn.shape
    wc = conv.shape[1]
    d = w_out.shape[1]
    assert wa == wc and row0 % tm == 0 and n % tm == 0
    r0 = row0 // tm
    in_specs = [pl.BlockSpec((tm, wa), lambda i, j: (i, 0)),
                pl.BlockSpec((tm, wc), lambda i, j: (i, 0)),
                pl.BlockSpec((wa, tn), lambda i, j: (0, j)),
                pl.BlockSpec((wc, tn), lambda i, j: (1, j)),
                pl.BlockSpec((tm, tn), lambda i, j: (i, j))]
    args = [attn, conv, w_out, w_out, h]
    aliases = {}
    if prev is not None:
        in_specs = [pl.BlockSpec(memory_space=pl.ANY)] + in_specs
        args = [prev] + args
        aliases = {0: 0}
    return pl.pallas_call(
        functools.partial(_outproj_kernel, aliased=prev is not None),
        out_shape=jax.ShapeDtypeStruct((n_total, d), F32),
        grid=(n // tm, d // tn),
        in_specs=in_specs,
        out_specs=pl.BlockSpec((tm, tn), lambda i, j: (r0 + i, j)),
        input_output_aliases=aliases,
        compiler_params=_params("parallel", "arbitrary"),
        name="outproj_residual",
    )(*args)


def _router_kernel(h_ref, g_ref, w_ref, b_ref, eid_ref, gate_ref, *, n_groups, per_group):
    x = h_ref[...]
    xn = (x * lax.rsqrt(jnp.mean(x * x, axis=-1, keepdims=True) + EPS) * g_ref[...]).astype(BF16)
    lg = jnp.dot(xn, w_ref[...], preferred_element_type=F32) + b_ref[...]
    lane = lax.broadcasted_iota(jnp.int32, lg.shape, 1)
    ninf = -jnp.inf

    def top1(vals):
        mx = jnp.max(vals, axis=-1, keepdims=True)
        idx = jnp.min(jnp.where(vals == mx, lane, LANES), axis=-1, keepdims=True)
        return mx, idx

    gl = jnp.where(lane < n_groups, lg, ninf)
    gmax, gidx = top1(gl)
    p_g = 1.0 / jnp.sum(jnp.exp(gl - gmax), axis=-1, keepdims=True)
    lo = n_groups + per_group * gidx
    el = jnp.where((lane >= lo) & (lane < lo + per_group), lg, ninf)
    m1, i1 = top1(el)
    m2, i2 = top1(jnp.where(lane == i1, ninf, el))
    e21 = jnp.exp(m2 - m1)
    g1 = p_g / (1.0 + e21)
    g2 = g1 * e21
    eid_ref[...] = jnp.where(lane == 0, i1 - n_groups, jnp.where(lane == 1, i2 - n_groups, 0))
    gate_ref[...] = jnp.where(lane == 0, g1, jnp.where(lane == 1, g2, 0.0))


def _router(h, g, w_r, b_r, n_groups, per_group, tm):
    n, d = h.shape
    return pl.pallas_call(
        functools.partial(_router_kernel, n_groups=n_groups, per_group=per_group),
        out_shape=[jax.ShapeDtypeStruct((n, LANES), jnp.int32), jax.ShapeDtypeStruct((n, LANES), F32)],
        grid=(n // tm,),
        in_specs=[pl.BlockSpec((tm, d), lambda i: (i, 0)),
                  pl.BlockSpec((1, d), lambda i: (0, 0)),
                  pl.BlockSpec((d, LANES), lambda i: (0, 0)),
                  pl.BlockSpec((1, LANES), lambda i: (0, 0))],
        out_specs=[pl.BlockSpec((tm, LANES), lambda i: (i, 0)), pl.BlockSpec((tm, LANES), lambda i: (i, 0))],
        compiler_params=_params("parallel"),
        name="router",
    )(h, g.reshape(1, d), w_r, b_r)


def _dispatch_tables(e_id, n_exp):
    n, k = e_id.shape
    m = n * k
    sb = MOE_SUB
    rc = sb * MOE_SUBS_PER_CHUNK
    n_chunks = _cdiv(m, rc) + n_exp
    n_sub_max = _cdiv(m, sb) + n_exp
    e_flat = e_id.reshape(m)
    onehot = (e_flat[:, None] == jnp.arange(n_exp, dtype=jnp.int32)[None, :]).astype(jnp.int32)
    csum = jnp.cumsum(onehot, axis=0)
    counts = csum[-1]
    rank = jnp.take_along_axis(csum, e_flat[:, None], axis=1)[:, 0] - 1
    cpe = (counts + rc - 1) // rc
    chunk_end = jnp.cumsum(cpe)
    chunk_start = chunk_end - cpe
    n_used = chunk_end[-1]
    dest = (chunk_start[e_flat] * rc + rank).astype(jnp.int32)
    c_ar = jnp.arange(n_chunks, dtype=jnp.int32)
    used = c_ar < n_used
    c_cl = jnp.minimum(c_ar, n_used - 1)
    chunk_e = jnp.minimum(jnp.searchsorted(chunk_end, c_cl, side='right'), n_exp - 1).astype(jnp.int32)
    rows = jnp.clip(counts[chunk_e] - (c_cl - chunk_start[chunk_e]) * rc, 0, rc)
    nsub = jnp.where(used, (rows + sb - 1) // sb, 0).astype(jnp.int32)
    in_blk = c_cl.astype(jnp.int32)
    out_blk = jnp.where(used, c_ar, n_chunks).astype(jnp.int32)
    tok = (jnp.arange(m, dtype=jnp.int32) // k)
    src = jnp.zeros((n_chunks * rc,), jnp.int32).at[dest].set(tok)
    sub_active = (jnp.arange(MOE_SUBS_PER_CHUNK, dtype=jnp.int32)[None, :] < nsub[:, None]).reshape(-1)
    n_act = jnp.sum(sub_active.astype(jnp.int32))
    sub_ids = jnp.nonzero(sub_active, size=n_sub_max, fill_value=0)[0].astype(jnp.int32)
    sub_ids = jnp.where(jnp.arange(n_sub_max) < n_act, sub_ids, sub_ids[n_act - 1])
    src_sub = src.reshape(n_chunks * MOE_SUBS_PER_CHUNK, sb)[sub_ids].reshape(-1)
    return dict(dest=dest, chunk_e=chunk_e, nsub=nsub, in_blk=in_blk, out_blk=out_blk,
                sub_ids=sub_ids, src_sub=src_sub, n_act=n_act.reshape(1).astype(jnp.int32),
                n_chunks=n_chunks, n_sub_max=n_sub_max)


def _gather_kernel(src_ref, sub_ref, nact_ref, h_hbm, g_ref, o_ref, buf, sem, *, sb):
    i = pl.program_id(0)
    n_act = nact_ref[0]

    def issue(step, slot):
        def body(r, carry):
            tok = src_ref[step * sb + r]
            pltpu.make_async_copy(h_hbm.at[pl.ds(tok, 1)], buf.at[slot, pl.ds(r, 1)], sem.at[slot]).start()
            return carry
        lax.fori_loop(0, sb, body, 0)

    @pl.when(i == 0)
    def _():
        issue(0, 0)

    @pl.when(i + 1 < n_act)
    def _():
        issue(i + 1, (i + 1) % 2)

    @pl.when(i < n_act)
    def _():
        slot = i % 2
        pltpu.make_async_copy(h_hbm.at[pl.ds(0, sb)], buf.at[slot], sem.at[slot]).wait()
        x = buf[slot]
        ms = jnp.mean(x * x, axis=-1, keepdims=True)
        o_ref[...] = (x * lax.rsqrt(ms + EPS) * g_ref[...]).astype(o_ref.dtype)


def _gather_rows(h, g, tabs):
    n, d = h.shape
    sb = MOE_SUB
    rows = tabs['n_chunks'] * sb * MOE_SUBS_PER_CHUNK
    return pl.pallas_call(
        functools.partial(_gather_kernel, sb=sb),
        out_shape=jax.ShapeDtypeStruct((rows, d), BF16),
        grid_spec=pltpu.PrefetchScalarGridSpec(
            num_scalar_prefetch=3,
            grid=(tabs['n_sub_max'],),
            in_specs=[pl.BlockSpec(memory_space=pl.ANY),
                      pl.BlockSpec((1, d), lambda i, src, sub, nact: (0, 0))],
            out_specs=pl.BlockSpec((sb, d), lambda i, src, sub, nact: (sub[i], 0)),
            scratch_shapes=[pltpu.VMEM((2, sb, d), F32), pltpu.SemaphoreType.DMA((2,))]),
        compiler_params=_params("arbitrary"),
        name="moe_gather",
    )(tabs['src_sub'], tabs['sub_ids'], tabs['n_act'], h, g.reshape(1, d))


def _moe_kernel(ce_ref, nsub_ref, ib_ref, ob_ref, x_ref, wg_ref, wu_ref, wd_ref, o_ref,
                hid_sc, wg_sc, wu_sc, wd_sc, *, sb, n_f, tf):
    c = pl.program_id(0)
    s = pl.program_id(1)
    nsub = nsub_ref[c]

    @pl.when((nsub > 0) & (s < n_f))
    def _():
        wg_sc[...] = wg_ref[0].astype(BF16)
        wu_sc[...] = wu_ref[0].astype(BF16)
        for j in range(MOE_SUBS_PER_CHUNK):
            @pl.when(j < nsub)
            def _():
                x = x_ref[j * sb:(j + 1) * sb, :]
                a = jnp.dot(x, wg_sc[...], preferred_element_type=F32)
                u = jnp.dot(x, wu_sc[...], preferred_element_type=F32)
                hid_sc[s, j * sb:(j + 1) * sb, :] = (a * jax.nn.sigmoid(a) * u).astype(BF16)

    @pl.when((nsub > 0) & (s >= n_f))
    def _():
        wd_sc[...] = wd_ref[0].astype(BF16)
        for j in range(MOE_SUBS_PER_CHUNK):
            @pl.when(j < nsub)
            def _():
                acc = jnp.dot(hid_sc[0, j * sb:(j + 1) * sb, :], wd_sc[0:tf, :], preferred_element_type=F32)
                for f in range(1, n_f):
                    acc = acc + jnp.dot(hid_sc[f, j * sb:(j + 1) * sb, :], wd_sc[f * tf:(f + 1) * tf, :],
                                        preferred_element_type=F32)
                o_ref[j * sb:(j + 1) * sb, :] = acc


def _moe_mlp(x_pad, w_gate, w_up, w_down, tabs):
    n_exp, d, ff = w_gate.shape
    sb = MOE_SUB
    rc = sb * MOE_SUBS_PER_CHUNK
    tf = 256
    td = 512
    n_f = ff // tf
    n_d = d // td
    n_chunks = tabs['n_chunks']
    wmap = lambda c, s, ce, ns, ib, ob: (ce[c], 0, jnp.minimum(s, n_f - 1))
    return pl.pallas_call(
        functools.partial(_moe_kernel, sb=sb, n_f=n_f, tf=tf),
        out_shape=jax.ShapeDtypeStruct(((n_chunks + 1) * rc, d), F32),
        grid_spec=pltpu.PrefetchScalarGridSpec(
            num_scalar_prefetch=4,
            grid=(n_chunks, n_f + n_d),
            in_specs=[pl.BlockSpec((rc, d), lambda c, s, ce, ns, ib, ob: (ib[c], 0)),
                      pl.BlockSpec((1, d, tf), wmap),
                      pl.BlockSpec((1, d, tf), wmap),
                      pl.BlockSpec((1, ff, td), lambda c, s, ce, ns, ib, ob: (ce[c], 0, jnp.maximum(s - n_f, 0)))],
            out_specs=pl.BlockSpec((rc, td), lambda c, s, ce, ns, ib, ob:
                                   (ob[c], jnp.where(ns[c] > 0, jnp.maximum(s - n_f, 0), 0))),
            scratch_shapes=[pltpu.VMEM((n_f, rc, tf), BF16), pltpu.VMEM((d, tf), BF16),
                            pltpu.VMEM((d, tf), BF16), pltpu.VMEM((ff, td), BF16)]),
        compiler_params=_params("arbitrary", "arbitrary"),
        name="moe_mlp",
    )(tabs['chunk_e'], tabs['nsub'], tabs['in_blk'], tabs['out_blk'], x_pad, w_gate, w_up, w_down)


def _combine_kernel(dest_ref, y_hbm, h_ref, gate_ref, o_ref, buf, sem, *, tm):
    i = pl.program_id(0)
    n = pl.num_programs(0)

    def issue(step, slot):
        def body(r, carry):
            for k in range(TOP_K):
                row = dest_ref[(step * tm + r) * TOP_K + k]
                pltpu.make_async_copy(y_hbm.at[pl.ds(row, 1)], buf.at[slot, k, pl.ds(r, 1)], sem.at[slot]).start()
            return carry
        lax.fori_loop(0, tm, body, 0)

    @pl.when(i == 0)
    def _():
        issue(0, 0)

    @pl.when(i + 1 < n)
    def _():
        issue(i + 1, (i + 1) % 2)

    slot = i % 2
    for k in range(TOP_K):
        pltpu.make_async_copy(y_hbm.at[pl.ds(0, tm)], buf.at[slot, k], sem.at[slot]).wait()
    gate = gate_ref[...]
    out = h_ref[...]
    for k in range(TOP_K):
        out = out + gate[:, k:k + 1] * buf[slot, k]
    o_ref[...] = out


def _combine(y_pad, h, gate, dest, tm):
    n, d = h.shape
    return pl.pallas_call(
        functools.partial(_combine_kernel, tm=tm),
        out_shape=jax.ShapeDtypeStruct((n, d), F32),
        grid_spec=pltpu.PrefetchScalarGridSpec(
            num_scalar_prefetch=1,
            grid=(n // tm,),
            in_specs=[pl.BlockSpec(memory_space=pl.ANY),
                      pl.BlockSpec((tm, d), lambda i, dest: (i, 0)),
                      pl.BlockSpec((tm, LANES), lambda i, dest: (i, 0))],
            out_specs=pl.BlockSpec((tm, d), lambda i, dest: (i, 0)),
            scratch_shapes=[pltpu.VMEM((2, TOP_K, tm, d), F32), pltpu.SemaphoreType.DMA((2,))]),
        compiler_params=_params("arbitrary"),
        name="moe_combine",
    )(dest, y_pad, h, gate)


def _mixer_inputs(h, norm_g, w_in_bf, wf, bf, q_g, k_g, conv_dim, attn_w, n_heads, tm):
    hd = attn_w // n_heads
    tn = 512
    xn = _rmsnorm(h, norm_g, _div_tile(h.shape[0], 256, 8))
    glu = _proj_glu(xn, w_in_bf, conv_dim, tm, tn)
    c1 = 2 * conv_dim
    (q,) = _proj_headnorm(xn, w_in_bf, q_g, c1, attn_w, hd, hd ** -0.5, (BF16,), tm, tn)
    k32, k16 = _proj_headnorm(xn, w_in_bf, k_g, c1 + attn_w, attn_w, hd, None, (F32, BF16), tm, tn)
    v32, v16 = _proj_plain(xn, w_in_bf, c1 + 2 * attn_w, attn_w, (F32, BF16), tm, tn)
    logf = _proj_logf(xn, wf, bf, tm)[:, :n_heads]
    return glu, q, k32, k16, v32, v16, logf


def kernel(x_prompt, x_sample, cache_k, cache_v, cache_logf, cache_conv, meta_tokens, norm_mix_g, w_in, b_forget,
           q_norm_g, k_norm_g, conv_w, conv_b, conv_ln_g, conv_ln_b, w_out, norm_ffn_g, w_router_group,
           b_router_group, w_router_expert, b_router_expert, w_gate, w_up, w_down):
    depth = w_in.shape[0]
    assert depth == 1
    bp, seq, d = x_prompt.shape
    bs, tq, _ = x_sample.shape
    n_heads = cache_k.shape[3]
    hd = cache_k.shape[4]
    attn_w = n_heads * hd
    conv_dim = conv_w.shape[2]
    hist_len = conv_w.shape[1] - 1
    past = cache_k.shape[2]
    n_groups = w_router_group.shape[2]
    n_exp = w_router_expert.shape[2]
    per_group = n_exp // n_groups
    assert n_groups + n_exp <= LANES and 2 * conv_dim % 512 == 0 and attn_w % 512 == 0
    t = N_META + seq
    npr = bp * t
    ns = bs * tq
    n_all = npr + ns

    w_in_bf = w_in[0].astype(BF16)
    c4 = 2 * conv_dim + 3 * attn_w
    wf = jnp.pad(w_in[0][:, c4:], ((0, 0), (0, LANES - n_heads))).astype(BF16)
    bf = jnp.pad(b_forget[0], (0, LANES - n_heads)).reshape(1, LANES)
    w_out_bf = w_out[0].astype(BF16)
    w_r = jnp.pad(jnp.concatenate([w_router_group[0], w_router_expert[0]], axis=1),
                  ((0, 0), (0, LANES - n_groups - n_exp))).astype(BF16)
    b_r = jnp.pad(jnp.concatenate([b_router_group[0], b_router_expert[0]]),
                  (0, LANES - n_groups - n_exp)).reshape(1, LANES)

    h_p = jnp.concatenate([jnp.broadcast_to(meta_tokens[None], (bp, N_META, d)), x_prompt], axis=1).reshape(npr, d)
    h_s = x_sample.reshape(ns, d)

    tm_p = _div_tile(npr, 768, 16)
    tm_s = _div_tile(ns, 768, 16)
    mix = functools.partial(_mixer_inputs, norm_g=norm_mix_g[0], w_in_bf=w_in_bf, wf=wf, bf=bf, q_g=q_norm_g[0],
                            k_g=k_norm_g[0], conv_dim=conv_dim, attn_w=attn_w, n_heads=n_heads)
    glu_p, q_p, k32_p, k16_p, v32_p, v16_p, logf_p = mix(h_p, tm=tm_p)
    glu_s, q_s, k32_s, k16_s, v32_s, v16_s, logf_s = mix(h_s, tm=tm_s)

    glu_p3 = glu_p.reshape(bp, t, conv_dim)
    glu_s3 = glu_s.reshape(bs, tq, conv_dim)
    hist_p = jnp.zeros((bp, hist_len + 2, conv_dim), F32)
    hist_s = jnp.pad(cache_conv[0], ((0, 0), (2, 0), (0, 0)))
    conv_args = (conv_w[0], conv_b[0], conv_ln_g[0], conv_ln_b[0])
    conv_p = _conv_module(glu_p3, hist_p, *conv_args)
    conv_s = _conv_module(glu_s3, hist_s, *conv_args)

    logf_p3 = logf_p.reshape(bp, t, n_heads)
    logf_s3 = logf_s.reshape(bs, tq, n_heads)
    blk = ATTN_BLOCK
    pad_t = lambda a: jnp.pad(a, ((0, 0), (0, _cdiv(a.shape[1], blk) * blk - a.shape[1]), (0, 0)))
    cum_p = _cumsum_time(pad_t(logf_p3), blk)[:, :t]
    lf_all = jnp.concatenate([cache_logf[0], logf_s3], axis=1)
    cum_s = _cumsum_time(pad_t(lf_all), blk)[:, :past + tq]
    r3 = lambda a, b_, t_: a.reshape(b_, t_, attn_w)
    attn_p = _attn_prompt(r3(q_p, bp, t), r3(k16_p, bp, t), r3(v16_p, bp, t), cum_p, n_heads)
    attn_s = _attn_sample(r3(q_s, bs, tq), cache_k[0].reshape(bs, past, attn_w), cache_v[0].reshape(bs, past, attn_w),
                          r3(k16_s, bs, tq), r3(v16_s, bs, tq), cum_s, n_heads)

    tn_o = 512
    h2 = _outproj(attn_p.reshape(npr, attn_w), conv_p.reshape(npr, conv_dim), w_out_bf, h_p, n_all, 0, tm_p, tn_o)
    tm_s2 = _div_tile(_gcd(npr, ns), 256, 16)
    h2 = _outproj(attn_s.reshape(ns, attn_w), conv_s.reshape(ns, conv_dim), w_out_bf, h_s, n_all, npr, tm_s2, tn_o,
                  prev=h2)

    tm_r = _div_tile(n_all, 128, 8)
    eid, gate = _router(h2, norm_ffn_g[0], w_r, b_r, n_groups, per_group, tm_r)
    tabs = _dispatch_tables(eid[:, :TOP_K], n_exp)
    x_pad = _gather_rows(h2, norm_ffn_g[0], tabs)
    y_pad = _moe_mlp(x_pad, w_gate[0], w_up[0], w_down[0], tabs)
    y = _combine(y_pad, h2, gate, tabs['dest'], tm_r)

    y_p = y[:npr].reshape(bp, t, d)[:, N_META:]
    y_s = y[npr:].reshape(bs, tq, d)
    st = lambda a, b_, t_: a.reshape(1, b_, t_, n_heads, hd)
    return (y_p, y_s,
            st(k32_p, bp, t), st(v32_p, bp, t), logf_p3[None], glu_p3[:, t - hist_len:][None],
            st(k32_s, bs, tq), st(v32_s, bs, tq), logf_s3[None],
            jnp.concatenate([cache_conv[0], glu_s3], axis=1)[:, -hist_len:][None])


def _gcd(a, b):
    while b:
        a, b = b, a % b
    return a
```

```python
import functools

import jax
import jax.numpy as jnp
from jax import lax
from jax.experimental import pallas as pl
from jax.experimental.pallas import tpu as pltpu

F32 = jnp.float32
BF16 = jnp.bfloat16
U32 = jnp.uint32

N_META = 16
CONV_GROUPS = 16
TOP_K = 2
EPS = 1e-6
NEG_INF = -1e30

LANES = 128
SUBLANES = 8
ATTN_BLOCK = 512
MOE_SUB = 256
MOE_SUBS_PER_CHUNK = 3
DMA_UNROLL = 8
VMEM_LIMIT = 56 * 1024 * 1024


def _cdiv(a, b):
    return -(-a // b)


def _gcd(a, b):
    while b:
        a, b = b, a % b
    return a


def _div_tile(n, target, mult):
    best = None
    for d in range(mult, min(n, target) + 1, mult):
        if n % d == 0:
            best = d
    assert best is not None, (n, target, mult)
    return best


def _params(*sem):
    return pltpu.CompilerParams(dimension_semantics=sem, vmem_limit_bytes=VMEM_LIMIT)


def _rmsnorm_kernel(x_ref, g_ref, o_ref):
    x = x_ref[...]
    ms = jnp.mean(x * x, axis=-1, keepdims=True)
    o_ref[...] = (x * lax.rsqrt(ms + EPS) * g_ref[...]).astype(o_ref.dtype)


def _rmsnorm(x, g, tm):
    n, d = x.shape
    return pl.pallas_call(
        _rmsnorm_kernel,
        out_shape=jax.ShapeDtypeStruct((n, d), BF16),
        grid=(n // tm,),
        in_specs=[pl.BlockSpec((tm, d), lambda i: (i, 0)), pl.BlockSpec((1, d), lambda i: (0, 0))],
        out_specs=pl.BlockSpec((tm, d), lambda i: (i, 0)),
        compiler_params=_params("parallel"),
        name="rmsnorm",
    )(x, g.reshape(1, d))


def _glu_kernel(x_ref, wa_ref, wb_ref, o_ref):
    x = x_ref[...]
    a = jnp.dot(x, wa_ref[...], preferred_element_type=F32)
    b = jnp.dot(x, wb_ref[...], preferred_element_type=F32)
    o_ref[...] = a * jax.nn.sigmoid(b)


def _proj_glu(xn, w, conv_dim, tm, tn):
    n, d = xn.shape
    nb = conv_dim // tn
    return pl.pallas_call(
        _glu_kernel,
        out_shape=jax.ShapeDtypeStruct((n, conv_dim), F32),
        grid=(n // tm, nb),
        in_specs=[pl.BlockSpec((tm, d), lambda i, j: (i, 0)),
                  pl.BlockSpec((d, tn), lambda i, j: (0, j)),
                  pl.BlockSpec((d, tn), lambda i, j: (0, nb + j))],
        out_specs=pl.BlockSpec((tm, tn), lambda i, j: (i, j)),
        compiler_params=_params("parallel", "arbitrary"),
        name="proj_glu",
    )(xn, w, w)


def _headnorm_kernel(x_ref, w_ref, g_ref, *o_refs, scale, hd):
    z = jnp.dot(x_ref[...], w_ref[...], preferred_element_type=F32)
    g = g_ref[...]
    for h in range(z.shape[1] // hd):
        zh = z[:, h * hd:(h + 1) * hd]
        y = zh * lax.rsqrt(jnp.mean(zh * zh, axis=-1, keepdims=True) + EPS) * g
        if scale is not None:
            y = y * scale
        for o_ref in o_refs:
            o_ref[:, h * hd:(h + 1) * hd] = y.astype(o_ref.dtype)


def _proj_headnorm(xn, w, g, col0, width, hd, scale, out_dtypes, tm, tn):
    n, d = xn.shape
    nb = width // tn
    b0 = col0 // tn
    outs = pl.pallas_call(
        functools.partial(_headnorm_kernel, scale=scale, hd=hd),
        out_shape=[jax.ShapeDtypeStruct((n, width), dt) for dt in out_dtypes],
        grid=(n // tm, nb),
        in_specs=[pl.BlockSpec((tm, d), lambda i, j: (i, 0)),
                  pl.BlockSpec((d, tn), lambda i, j: (0, b0 + j)),
                  pl.BlockSpec((1, hd), lambda i, j: (0, 0))],
        out_specs=[pl.BlockSpec((tm, tn), lambda i, j: (i, j)) for _ in out_dtypes],
        compiler_params=_params("parallel", "arbitrary"),
        name="proj_headnorm",
    )(xn, w, g.reshape(1, hd))
    return outs


def _plain_kernel(x_ref, w_ref, *o_refs):
    z = jnp.dot(x_ref[...], w_ref[...], preferred_element_type=F32)
    for o_ref in o_refs:
        o_ref[...] = z.astype(o_ref.dtype)


def _proj_plain(xn, w, col0, width, out_dtypes, tm, tn):
    n, d = xn.shape
    nb = width // tn
    b0 = col0 // tn
    return pl.pallas_call(
        _plain_kernel,
        out_shape=[jax.ShapeDtypeStruct((n, width), dt) for dt in out_dtypes],
        grid=(n // tm, nb),
        in_specs=[pl.BlockSpec((tm, d), lambda i, j: (i, 0)),
                  pl.BlockSpec((d, tn), lambda i, j: (0, b0 + j))],
        out_specs=[pl.BlockSpec((tm, tn), lambda i, j: (i, j)) for _ in out_dtypes],
        compiler_params=_params("parallel", "arbitrary"),
        name="proj_plain",
    )(xn, w)


def _logf_kernel(x_ref, w_ref, b_ref, o_ref):
    z = jnp.dot(x_ref[...], w_ref[...], preferred_element_type=F32) + b_ref[...]
    o_ref[...] = jnp.minimum(z, 0.0) - jnp.log1p(jnp.exp(-jnp.abs(z)))


def _proj_logf(xn, wf, bf, tm):
    n, d = xn.shape
    return pl.pallas_call(
        _logf_kernel,
        out_shape=jax.ShapeDtypeStruct((n, LANES), F32),
        grid=(n // tm,),
        in_specs=[pl.BlockSpec((tm, d), lambda i: (i, 0)),
                  pl.BlockSpec((d, LANES), lambda i: (0, 0)),
                  pl.BlockSpec((1, LANES), lambda i: (0, 0))],
        out_specs=pl.BlockSpec((tm, LANES), lambda i: (i, 0)),
        compiler_params=_params("parallel"),
        name="proj_logf",
    )(xn, wf, bf)


def _conv_kernel(glu_ref, hist_ref, cw_ref, cb_ref, lg_ref, lb_ref, o_ref, full_sc, *, t, tt, width):
    hp = hist_ref.shape[1]
    lanes = glu_ref.shape[2]
    full_sc[0:hp, :] = hist_ref[0]
    full_sc[hp:hp + t, :] = glu_ref[0]
    full_sc[hp + t:hp + t + SUBLANES, :] = jnp.zeros((SUBLANES, lanes), F32)

    def chunk(ci, carry):
        t0 = pl.multiple_of(ci * tt, 16)
        for lt in range(lanes // LANES):
            ls = slice(lt * LANES, (lt + 1) * LANES)
            win = full_sc[pl.ds(t0, tt + hp + SUBLANES), ls]
            y = None
            for r in range(SUBLANES):
                part = None
                for a in range(_cdiv(width + 2, SUBLANES)):
                    j = SUBLANES * a + r
                    if not 2 <= j < width + 2:
                        continue
                    term = win[SUBLANES * a:SUBLANES * a + tt + SUBLANES, :] * cw_ref[j - 2:j - 1, ls]
                    part = term if part is None else part + term
                if part is None:
                    continue
                shifted = part[r:r + tt, :]
                y = shifted if y is None else y + shifted
            y = y + cb_ref[:, ls]
            mu = jnp.mean(y, axis=-1, keepdims=True)
            dlt = y - mu
            var = jnp.mean(dlt * dlt, axis=-1, keepdims=True)
            yn = dlt * lax.rsqrt(var + EPS) * lg_ref[:, ls] + lb_ref[:, ls]
            o_ref[0, pl.ds(t0, tt), ls] = (yn * jax.nn.sigmoid(yn)).astype(o_ref.dtype)
        return carry

    lax.fori_loop(0, t // tt, chunk, 0)


def _conv_module(glu, hist, conv_w, conv_b, ln_g, ln_b):
    b, t, c = glu.shape
    width = conv_w.shape[0]
    hp = hist.shape[1]
    assert hp == 32 and width + 1 == hp and c // CONV_GROUPS == LANES
    tt = _div_tile(t, 64, 16)
    tc = 2 * LANES
    cw = jnp.concatenate([conv_w, jnp.zeros((hp - width, c), F32)], axis=0)
    vec = lambda v: v.reshape(1, c)
    return pl.pallas_call(
        functools.partial(_conv_kernel, t=t, tt=tt, width=width),
        out_shape=jax.ShapeDtypeStruct((b, t, c), BF16),
        grid=(b, c // tc),
        in_specs=[pl.BlockSpec((1, t, tc), lambda i, j: (i, 0, j)),
                  pl.BlockSpec((1, hp, tc), lambda i, j: (i, 0, j)),
                  pl.BlockSpec((hp, tc), lambda i, j: (0, j)),
                  pl.BlockSpec((1, tc), lambda i, j: (0, j)),
                  pl.BlockSpec((1, tc), lambda i, j: (0, j)),
                  pl.BlockSpec((1, tc), lambda i, j: (0, j))],
        out_specs=pl.BlockSpec((1, t, tc), lambda i, j: (i, 0, j)),
        scratch_shapes=[pltpu.VMEM((hp + t + SUBLANES, tc), F32)],
        compiler_params=_params("parallel", "parallel"),
        name="conv_ln_swish",
    )(glu, hist, cw, vec(conv_b), vec(ln_g), vec(ln_b))


def _cumsum_kernel(x_ref, o_ref, *, nblk, blk):
    r = lax.broadcasted_iota(jnp.int32, (blk, blk), 0)
    c = lax.broadcasted_iota(jnp.int32, (blk, blk), 1)
    tri = (c <= r).astype(F32)
    carry = jnp.zeros((1, x_ref.shape[2]), F32)
    for i in range(nblk):
        x = x_ref[0, i * blk:(i + 1) * blk, :]
        o_ref[0, i * blk:(i + 1) * blk, :] = jnp.dot(
            tri, x, precision=lax.Precision.HIGHEST, preferred_element_type=F32) + carry
        carry = carry + jnp.sum(x, axis=0, keepdims=True)


def _cumsum_time(x, blk):
    b, t, h = x.shape
    return pl.pallas_call(
        functools.partial(_cumsum_kernel, nblk=t // blk, blk=blk),
        out_shape=jax.ShapeDtypeStruct((b, t, h), F32),
        grid=(b,),
        in_specs=[pl.BlockSpec((1, t, h), lambda i: (i, 0, 0))],
        out_specs=pl.BlockSpec((1, t, h), lambda i: (i, 0, 0)),
        compiler_params=_params("parallel"),
        name="logf_cumsum",
    )(x)


def _attn_update(q, k, v, cq, ck, mask, m, l, acc):
    s = lax.dot_general(q, k, (((1,), (1,)), ((), ())), preferred_element_type=F32)
    s = s - ck
    if mask is not None:
        s = jnp.where(mask, s, NEG_INF)
    m_new = jnp.maximum(m, jnp.max(s, axis=-1, keepdims=True) + cq)
    alpha = jnp.exp(m - m_new)
    p = jnp.exp(s + (cq - m_new))
    l_new = alpha * l + jnp.sum(p, axis=-1, keepdims=True)
    acc_new = alpha * acc + jnp.dot(p.astype(BF16), v, preferred_element_type=F32)
    return m_new, l_new, acc_new


def _softmax_init(tq, d):
    return (jnp.full((tq, 1), NEG_INF, F32), jnp.zeros((tq, 1), F32), jnp.zeros((tq, d), F32))


def _head_column(c, h):
    sel = lax.broadcasted_iota(jnp.int32, c.shape, 1) == h
    return jnp.sum(jnp.where(sel, c, 0.0), axis=-1, keepdims=True)


def _causal_mask(n):
    r = lax.broadcasted_iota(jnp.int32, (n, n), 0)
    c = lax.broadcasted_iota(jnp.int32, (n, n), 1)
    return c <= r


def _attn_prompt_kernel(q_ref, k_ref, v_ref, cum_ref, cumt_ref, o_ref, s_sc, d_sc, p_sc, pd_sc, l_sc, *, t, bq, rs):
    h = pl.program_id(1)
    nfull = t // bq
    tail = t - nfull * bq
    nt = (((1,), (1,)), ((), ()))
    blocks = [(i * bq, bq) for i in range(nfull)] + ([(nfull * bq, tail)] if tail else [])
    for q0, nq in blocks:
        q = q_ref[0, q0:q0 + nq, :]
        if q0:
            s_sc[0:nq, 0:q0] = lax.dot_general(q, k_ref[0, 0:q0, :], nt, preferred_element_type=F32)
        d_sc[0:nq, 0:nq] = lax.dot_general(q, k_ref[0, q0:q0 + nq, :], nt, preferred_element_type=F32)
        ck_past = cumt_ref[0, 0, :, 0:q0] if q0 else None
        ck_diag = cumt_ref[0, 0, :, q0:q0 + nq]
        rows = min(rs, nq)

        def softmax_rows(ri, carry, q0=q0, nq=nq, rows=rows, ck_past=ck_past, ck_diag=ck_diag):
            r0 = pl.multiple_of(ri * rows, rows)
            cq = _head_column(cum_ref[0, pl.ds(q0 + r0, rows), :], h)
            rr = lax.broadcasted_iota(jnp.int32, (rows, nq), 0) + r0
            cc = lax.broadcasted_iota(jnp.int32, (rows, nq), 1)
            sd = jnp.where(cc <= rr, d_sc[pl.ds(r0, rows), 0:nq] - ck_diag, NEG_INF)
            mx = jnp.max(sd, axis=-1, keepdims=True)
            if q0:
                sp = s_sc[pl.ds(r0, rows), 0:q0] - ck_past
                mx = jnp.maximum(mx, jnp.max(sp, axis=-1, keepdims=True))
            shift = cq - (mx + cq)
            pd = jnp.exp(sd + shift)
            l = jnp.sum(pd, axis=-1, keepdims=True)
            pd_sc[pl.ds(r0, rows), 0:nq] = pd.astype(BF16)
            if q0:
                pp = jnp.exp(sp + shift)
                l = l + jnp.sum(pp, axis=-1, keepdims=True)
                p_sc[pl.ds(r0, rows), 0:q0] = pp.astype(BF16)
            l_sc[pl.ds(r0, rows), :] = l
            return carry

        lax.fori_loop(0, nq // rows, softmax_rows, 0)
        o = jnp.dot(pd_sc[0:nq, 0:nq], v_ref[0, q0:q0 + nq, :], preferred_element_type=F32)
        if q0:
            o = o + jnp.dot(p_sc[0:nq, 0:q0], v_ref[0, 0:q0, :], preferred_element_type=F32)
        o_ref[0, q0:q0 + nq, :] = (o / l_sc[0:nq, :]).astype(o_ref.dtype)


def _blocked_rows(cum, blk):
    b, t, h = cum.shape
    nblk = _cdiv(t, blk)
    c = jnp.pad(cum, ((0, 0), (0, nblk * blk - t), (0, 0)))
    return c.transpose(0, 2, 1).reshape(b, h, nblk, 1, blk)


def _attn_prompt(q, k, v, cum, n_heads):
    b, t, w = q.shape
    d = w // n_heads
    bq = ATTN_BLOCK
    nfull = t // bq
    assert nfull >= 1 and (t % bq) % 16 == 0
    tp = _cdiv(t, LANES) * LANES
    cumt = jnp.pad(cum, ((0, 0), (0, tp - t), (0, 0))).transpose(0, 2, 1).reshape(b, n_heads, 1, tp)
    blk = lambda: pl.BlockSpec((1, t, d), lambda i, j: (i, 0, j))
    return pl.pallas_call(
        functools.partial(_attn_prompt_kernel, t=t, bq=bq, rs=128),
        out_shape=jax.ShapeDtypeStruct((b, t, w), BF16),
        grid=(b, n_heads),
        in_specs=[blk(), blk(), blk(),
                  pl.BlockSpec((1, t, n_heads), lambda i, j: (i, 0, 0)),
                  pl.BlockSpec((1, 1, 1, tp), lambda i, j: (i, j, 0, 0))],
        out_specs=blk(),
        scratch_shapes=[pltpu.VMEM((bq, nfull * bq), F32), pltpu.VMEM((bq, bq), F32),
                        pltpu.VMEM((bq, nfull * bq), BF16), pltpu.VMEM((bq, bq), BF16),
                        pltpu.VMEM((bq, 1), F32)],
        compiler_params=_params("parallel", "parallel"),
        name="fox_attn_prompt",
    )(q, k, v, cum, cumt)


def _attn_sample_kernel(q_ref, ck_ref, cv_ref, k_ref, v_ref, cum_ref, cumt_ref, o_ref, *, past, bk):
    h = pl.program_id(1)
    tq, d = q_ref.shape[1], q_ref.shape[2]
    q = q_ref[0]
    cq = _head_column(cum_ref[0], h)
    carry = _softmax_init(tq, d)
    for kj in range(past // bk):
        kb = ck_ref[0, kj * bk:(kj + 1) * bk, :].astype(BF16)
        vb = cv_ref[0, kj * bk:(kj + 1) * bk, :].astype(BF16)
        carry = _attn_update(q, kb, vb, cq, cumt_ref[0, 0, kj], None, *carry)
    carry = _attn_update(q, k_ref[0], v_ref[0], cq, cumt_ref[0, 0, past // bk][:, :tq], _causal_mask(tq), *carry)
    _, l, acc = carry
    o_ref[0] = (acc / l).astype(o_ref.dtype)


def _attn_sample(q, cache_k, cache_v, k, v, cum, n_heads):
    b, tq, w = q.shape
    d = w // n_heads
    past = cache_k.shape[1]
    bk = _div_tile(past, ATTN_BLOCK, LANES)
    assert tq <= bk
    cumt = _blocked_rows(cum, bk)
    nblk = cumt.shape[2]
    cum_q = cum[:, past:]
    new = lambda: pl.BlockSpec((1, tq, d), lambda i, j: (i, 0, j))
    old = lambda: pl.BlockSpec((1, past, d), lambda i, j: (i, 0, j))
    return pl.pallas_call(
        functools.partial(_attn_sample_kernel, past=past, bk=bk),
        out_shape=jax.ShapeDtypeStruct((b, tq, w), BF16),
        grid=(b, n_heads),
        in_specs=[new(), old(), old(), new(), new(),
                  pl.BlockSpec((1, tq, n_heads), lambda i, j: (i, 0, 0)),
                  pl.BlockSpec((1, 1, nblk, 1, bk), lambda i, j: (i, j, 0, 0, 0))],
        out_specs=new(),
        compiler_params=_params("parallel", "parallel"),
        name="fox_attn_sample",
    )(q, cache_k, cache_v, k, v, cum_q, cumt)


def _outproj_kernel(*refs, aliased):
    if aliased:
        refs = refs[1:]
    a_ref, c_ref, wa_ref, wc_ref, h_ref, o_ref = refs
    o_ref[...] = (h_ref[...]
                  + jnp.dot(a_ref[...], wa_ref[...], preferred_element_type=F32)
                  + jnp.dot(c_ref[...], wc_ref[...], preferred_element_type=F32))


def _outproj(attn, conv, w_out, h, n_total, row0, tm, tn, prev=None):
    n, wa = attn.shape
    wc = conv.shape[1]
    d = w_out.shape[1]
    assert wa == wc and row0 % tm == 0 and n % tm == 0
    r0 = row0 // tm
    in_specs = [pl.BlockSpec((tm, wa), lambda i, j: (i, 0)),
                pl.BlockSpec((tm, wc), lambda i, j: (i, 0)),
                pl.BlockSpec((wa, tn), lambda i, j: (0, j)),
                pl.BlockSpec((wc, tn), lambda i, j: (1, j)),
                pl.BlockSpec((tm, tn), lambda i, j: (i, j))]
    args = [attn, conv, w_out, w_out, h]
    aliases = {}
    if prev is not None:
        in_specs = [pl.BlockSpec(memory_space=pl.ANY)] + in_specs
        args = [prev] + args
        aliases = {0: 0}
    return pl.pallas_call(
        functools.partial(_outproj_kernel, aliased=prev is not None),
        out_shape=jax.ShapeDtypeStruct((n_total, d), F32),
        grid=(n // tm, d // tn),
        in_specs=in_specs,
        out_specs=pl.BlockSpec((tm, tn), lambda i, j: (r0 + i, j)),
        input_output_aliases=aliases,
        compiler_params=_params("parallel", "arbitrary"),
        name="outproj_residual",
    )(*args)


def _router_kernel(h_ref, g_ref, w_ref, b_ref, eid_ref, gate_ref, xp_ref, *, n_groups, per_group):
    x = h_ref[...]
    xn = (x * lax.rsqrt(jnp.mean(x * x, axis=-1, keepdims=True) + EPS) * g_ref[...]).astype(BF16)
    half = x.shape[1] // 2
    bits = pltpu.bitcast(xn.astype(F32), U32)
    xp_ref[...] = bits[:, :half] | (bits[:, half:] >> 16)

    lg = jnp.dot(xn, w_ref[...], preferred_element_type=F32) + b_ref[...]
    lane = lax.broadcasted_iota(jnp.int32, lg.shape, 1)
    ninf = -jnp.inf

    def top1(vals):
        mx = jnp.max(vals, axis=-1, keepdims=True)
        idx = jnp.min(jnp.where(vals == mx, lane, LANES), axis=-1, keepdims=True)
        return mx, idx

    gl = jnp.where(lane < n_groups, lg, ninf)
    gmax, gidx = top1(gl)
    p_g = 1.0 / jnp.sum(jnp.exp(gl - gmax), axis=-1, keepdims=True)
    lo = n_groups + per_group * gidx
    el = jnp.where((lane >= lo) & (lane < lo + per_group), lg, ninf)
    m1, i1 = top1(el)
    m2, i2 = top1(jnp.where(lane == i1, ninf, el))
    e21 = jnp.exp(m2 - m1)
    g1 = p_g / (1.0 + e21)
    g2 = g1 * e21
    eid_ref[...] = jnp.where(lane == 0, i1 - n_groups, jnp.where(lane == 1, i2 - n_groups, 0))
    gate_ref[...] = jnp.where(lane == 0, g1, jnp.where(lane == 1, g2, 0.0))


def _router(h, g, w_r, b_r, n_groups, per_group, tm):
    n, d = h.shape
    row = lambda w: pl.BlockSpec((tm, w), lambda i: (i, 0))
    return pl.pallas_call(
        functools.partial(_router_kernel, n_groups=n_groups, per_group=per_group),
        out_shape=[jax.ShapeDtypeStruct((n, LANES), jnp.int32), jax.ShapeDtypeStruct((n, LANES), F32),
                   jax.ShapeDtypeStruct((n, d // 2), U32)],
        grid=(n // tm,),
        in_specs=[row(d),
                  pl.BlockSpec((1, d), lambda i: (0, 0)),
                  pl.BlockSpec((d, LANES), lambda i: (0, 0)),
                  pl.BlockSpec((1, LANES), lambda i: (0, 0))],
        out_specs=[row(LANES), row(LANES), row(d // 2)],
        compiler_params=_params("parallel"),
        name="router",
    )(h, g.reshape(1, d), w_r, b_r)


def _dispatch_tables(e_id, n_exp):
    n, k = e_id.shape
    m = n * k
    sb = MOE_SUB
    rc = sb * MOE_SUBS_PER_CHUNK
    n_chunks = _cdiv(m, rc) + n_exp
    n_sub_max = _cdiv(m, sb) + n_exp
    e_flat = e_id.reshape(m)
    onehot = (e_flat[:, None] == jnp.arange(n_exp, dtype=jnp.int32)[None, :]).astype(jnp.int32)
    csum = jnp.cumsum(onehot, axis=0)
    counts = csum[-1]
    rank = jnp.take_along_axis(csum, e_flat[:, None], axis=1)[:, 0] - 1
    cpe = (counts + rc - 1) // rc
    chunk_end = jnp.cumsum(cpe)
    chunk_start = chunk_end - cpe
    n_used = chunk_end[-1]
    dest = (chunk_start[e_flat] * rc + rank).astype(jnp.int32)
    c_ar = jnp.arange(n_chunks, dtype=jnp.int32)
    used = c_ar < n_used
    c_cl = jnp.minimum(c_ar, n_used - 1)
    chunk_e = jnp.minimum(jnp.searchsorted(chunk_end, c_cl, side='right'), n_exp - 1).astype(jnp.int32)
    rows = jnp.clip(counts[chunk_e] - (c_cl - chunk_start[chunk_e]) * rc, 0, rc)
    nsub = jnp.where(used, (rows + sb - 1) // sb, 0).astype(jnp.int32)
    in_blk = c_cl.astype(jnp.int32)
    out_blk = jnp.where(used, c_ar, n_chunks).astype(jnp.int32)
    tok = (jnp.arange(m, dtype=jnp.int32) // k)
    src = jnp.zeros((n_chunks * rc,), jnp.int32).at[dest].set(tok)
    sub_active = (jnp.arange(MOE_SUBS_PER_CHUNK, dtype=jnp.int32)[None, :] < nsub[:, None]).reshape(-1)
    n_act = jnp.sum(sub_active.astype(jnp.int32))
    sub_ids = jnp.nonzero(sub_active, size=n_sub_max, fill_value=0)[0].astype(jnp.int32)
    sub_ids = jnp.where(jnp.arange(n_sub_max) < n_act, sub_ids, sub_ids[n_act - 1])
    src_sub = src.reshape(n_chunks * MOE_SUBS_PER_CHUNK, sb)[sub_ids].reshape(-1)
    return dict(dest=dest, chunk_e=chunk_e, nsub=nsub, in_blk=in_blk, out_blk=out_blk,
                sub_ids=sub_ids, src_sub=src_sub, n_act=n_act.reshape(1).astype(jnp.int32),
                n_chunks=n_chunks, n_sub_max=n_sub_max)


def _gather_kernel(src_ref, sub_ref, nact_ref, xp_hbm, o_ref, buf, sem, *, sb):
    i = pl.program_id(0)
    n_act = nact_ref[0]
    half = xp_hbm.shape[1]

    def issue(step, slot):
        def body(it, carry):
            for u in range(DMA_UNROLL):
                r = it * DMA_UNROLL + u
                tok = src_ref[step * sb + r]
                pltpu.make_async_copy(xp_hbm.at[pl.ds(tok, 1)], buf.at[slot, pl.ds(r, 1)], sem.at[slot]).start()
            return carry
        lax.fori_loop(0, sb // DMA_UNROLL, body, 0)

    @pl.when(i == 0)
    def _():
        issue(0, 0)

    @pl.when(i + 1 < n_act)
    def _():
        issue(i + 1, (i + 1) % 2)

    @pl.when(i < n_act)
    def _():
        slot = i % 2
        pltpu.make_async_copy(xp_hbm.at[pl.ds(0, sb)], buf.at[slot], sem.at[slot]).wait()
        rows = 32
        for c in range(sb // rows):
            u = buf[slot, c * rows:(c + 1) * rows, :]
            hi = pltpu.bitcast(u & jnp.uint32(0xFFFF0000), F32)
            lo = pltpu.bitcast(u << 16, F32)
            o_ref[c * rows:(c + 1) * rows, 0:half] = hi.astype(o_ref.dtype)
            o_ref[c * rows:(c + 1) * rows, half:2 * half] = lo.astype(o_ref.dtype)


def _gather_rows(xp, tabs):
    n, half = xp.shape
    sb = MOE_SUB
    rows = tabs['n_chunks'] * sb * MOE_SUBS_PER_CHUNK
    return pl.pallas_call(
        functools.partial(_gather_kernel, sb=sb),
        out_shape=jax.ShapeDtypeStruct((rows, 2 * half), BF16),
        grid_spec=pltpu.PrefetchScalarGridSpec(
            num_scalar_prefetch=3,
            grid=(tabs['n_sub_max'],),
            in_specs=[pl.BlockSpec(memory_space=pl.ANY)],
            out_specs=pl.BlockSpec((sb, 2 * half), lambda i, src, sub, nact: (sub[i], 0)),
            scratch_shapes=[pltpu.VMEM((2, sb, half), U32), pltpu.SemaphoreType.DMA((2,))]),
        compiler_params=_params("arbitrary"),
        name="moe_gather",
    )(tabs['src_sub'], tabs['sub_ids'], tabs['n_act'], xp)


def _moe_kernel(ce_ref, nsub_ref, ib_ref, ob_ref, x_ref, wg_ref, wu_ref, wd_ref, o_ref, hid_sc, *, sb, n_f, tf):
    c = pl.program_id(0)
    s = pl.program_id(1)
    nsub = nsub_ref[c]

    for r in range(1, MOE_SUBS_PER_CHUNK + 1):
        rows = r * sb

        @pl.when((nsub == r) & (s < n_f))
        def _():
            x = x_ref[0:rows, :]
            g = jnp.dot(x, wg_ref[0].astype(BF16), preferred_element_type=F32)
            u = jnp.dot(x, wu_ref[0].astype(BF16), preferred_element_type=F32)
            hid = (g * jax.nn.sigmoid(g) * u).astype(BF16)
            for f in range(n_f):
                @pl.when(s == f)
                def _():
                    hid_sc[0:rows, f * tf:(f + 1) * tf] = hid

        @pl.when((nsub == r) & (s >= n_f))
        def _():
            o_ref[0:rows, :] = jnp.dot(hid_sc[0:rows, :], wd_ref[0].astype(BF16), preferred_element_type=F32)


def _moe_mlp(x_pad, w_gate, w_up, w_down, tabs):
    n_exp, d, ff = w_gate.shape
    sb = MOE_SUB
    rc = sb * MOE_SUBS_PER_CHUNK
    tf = 256
    td = 1024
    n_f = ff // tf
    n_d = d // td
    n_chunks = tabs['n_chunks']
    f_idx = lambda c, s, ns: jnp.where(ns[c] > 0, jnp.minimum(s, n_f - 1), n_f - 1)
    d_idx = lambda c, s, ns: jnp.where(ns[c] > 0, jnp.maximum(s - n_f, 0), n_d - 1)
    wmap = lambda c, s, ce, ns, ib, ob: (ce[c], 0, f_idx(c, s, ns))
    return pl.pallas_call(
        functools.partial(_moe_kernel, sb=sb, n_f=n_f, tf=tf),
        out_shape=jax.ShapeDtypeStruct(((n_chunks + 1) * rc, d), F32),
        grid_spec=pltpu.PrefetchScalarGridSpec(
            num_scalar_prefetch=4,
            grid=(n_chunks, n_f + n_d),
            in_specs=[pl.BlockSpec((rc, d), lambda c, s, ce, ns, ib, ob: (ib[c], 0)),
                      pl.BlockSpec((1, d, tf), wmap),
                      pl.BlockSpec((1, d, tf), wmap),
                      pl.BlockSpec((1, ff, td), lambda c, s, ce, ns, ib, ob: (ce[c], 0, d_idx(c, s, ns)))],
            out_specs=pl.BlockSpec((rc, td), lambda c, s, ce, ns, ib, ob:
                                   (ob[c], jnp.where(ns[c] > 0, jnp.maximum(s - n_f, 0), 0))),
            scratch_shapes=[pltpu.VMEM((rc, ff), BF16)]),
        compiler_params=_params("arbitrary", "arbitrary"),
        name="moe_mlp",
    )(tabs['chunk_e'], tabs['nsub'], tabs['in_blk'], tabs['out_blk'], x_pad, w_gate, w_up, w_down)


def _combine_kernel(dest_ref, row0_ref, y_hbm, h_hbm, gate_hbm, o_ref, ybuf, hbuf, gbuf, sem, *, tm):
    i = pl.program_id(0)
    n = pl.num_programs(0)

    def contiguous(step, slot):
        r0 = pl.multiple_of(row0_ref[step], SUBLANES)
        return (pltpu.make_async_copy(h_hbm.at[pl.ds(r0, tm)], hbuf.at[slot], sem.at[slot]),
                pltpu.make_async_copy(gate_hbm.at[pl.ds(r0, tm)], gbuf.at[slot], sem.at[slot]))

    def issue(step, slot):
        for cp in contiguous(step, slot):
            cp.start()
        base = row0_ref[step] * TOP_K

        def body(it, carry):
            for u in range(DMA_UNROLL // TOP_K):
                r = it * (DMA_UNROLL // TOP_K) + u
                for k in range(TOP_K):
                    row = dest_ref[base + r * TOP_K + k]
                    pltpu.make_async_copy(y_hbm.at[pl.ds(row, 1)], ybuf.at[slot, k, pl.ds(r, 1)],
                                          sem.at[slot]).start()
            return carry
        lax.fori_loop(0, tm * TOP_K // DMA_UNROLL, body, 0)

    @pl.when(i == 0)
    def _():
        issue(0, 0)

    @pl.when(i + 1 < n)
    def _():
        issue(i + 1, (i + 1) % 2)

    slot = i % 2
    for cp in contiguous(i, slot):
        cp.wait()
    for k in range(TOP_K):
        pltpu.make_async_copy(y_hbm.at[pl.ds(0, tm)], ybuf.at[slot, k], sem.at[slot]).wait()
    gate = gbuf[slot]
    out = hbuf[slot]
    for k in range(TOP_K):
        out = out + gate[:, k:k + 1] * ybuf[slot, k]
    o_ref[...] = out


def _combine(y_pad, h, gate, dest, row0, tm):
    d = h.shape[1]
    n_tiles = row0.shape[0]
    return pl.pallas_call(
        functools.partial(_combine_kernel, tm=tm),
        out_shape=jax.ShapeDtypeStruct((n_tiles * tm, d), F32),
        grid_spec=pltpu.PrefetchScalarGridSpec(
            num_scalar_prefetch=2,
            grid=(n_tiles,),
            in_specs=[pl.BlockSpec(memory_space=pl.ANY), pl.BlockSpec(memory_space=pl.ANY),
                      pl.BlockSpec(memory_space=pl.ANY)],
            out_specs=pl.BlockSpec((tm, d), lambda i, dest, row0: (i, 0)),
            scratch_shapes=[pltpu.VMEM((2, TOP_K, tm, d), F32), pltpu.VMEM((2, tm, d), F32),
                            pltpu.VMEM((2, tm, LANES), F32), pltpu.SemaphoreType.DMA((2,))]),
        compiler_params=_params("arbitrary"),
        name="moe_combine",
    )(dest, row0, y_pad, h, gate)


def _mixer_inputs(h, norm_g, w_in_bf, wf, bf, q_g, k_g, conv_dim, attn_w, n_heads, tm):
    hd = attn_w // n_heads
    tn = 512
    xn = _rmsnorm(h, norm_g, _div_tile(h.shape[0], 256, 8))
    glu = _proj_glu(xn, w_in_bf, conv_dim, tm, tn // 2)
    c1 = 2 * conv_dim
    (q,) = _proj_headnorm(xn, w_in_bf, q_g, c1, attn_w, hd, hd ** -0.5, (BF16,), tm, tn)
    k32, k16 = _proj_headnorm(xn, w_in_bf, k_g, c1 + attn_w, attn_w, hd, None, (F32, BF16), tm, tn)
    v32, v16 = _proj_plain(xn, w_in_bf, c1 + 2 * attn_w, attn_w, (F32, BF16), tm, tn)
    logf = _proj_logf(xn, wf, bf, tm)[:, :n_heads]
    return glu, q, k32, k16, v32, v16, logf


def kernel(x_prompt, x_sample, cache_k, cache_v, cache_logf, cache_conv, meta_tokens, norm_mix_g, w_in, b_forget,
           q_norm_g, k_norm_g, conv_w, conv_b, conv_ln_g, conv_ln_b, w_out, norm_ffn_g, w_router_group,
           b_router_group, w_router_expert, b_router_expert, w_gate, w_up, w_down):
    depth = w_in.shape[0]
    assert depth == 1
    bp, seq, d = x_prompt.shape
    bs, tq, _ = x_sample.shape
    n_heads = cache_k.shape[3]
    hd = cache_k.shape[4]
    attn_w = n_heads * hd
    conv_dim = conv_w.shape[2]
    hist_len = conv_w.shape[1] - 1
    past = cache_k.shape[2]
    n_groups = w_router_group.shape[2]
    n_exp = w_router_expert.shape[2]
    per_group = n_exp // n_groups
    assert n_groups + n_exp <= LANES and 2 * conv_dim % 512 == 0 and attn_w % 512 == 0
    t = N_META + seq
    npr = bp * t
    ns = bs * tq
    n_all = npr + ns

    w_in_bf = w_in[0].astype(BF16)
    c4 = 2 * conv_dim + 3 * attn_w
    wf = jnp.pad(w_in[0][:, c4:], ((0, 0), (0, LANES - n_heads))).astype(BF16)
    bf = jnp.pad(b_forget[0], (0, LANES - n_heads)).reshape(1, LANES)
    w_out_bf = w_out[0].astype(BF16)
    w_r = jnp.pad(jnp.concatenate([w_router_group[0], w_router_expert[0]], axis=1),
                  ((0, 0), (0, LANES - n_groups - n_exp))).astype(BF16)
    b_r = jnp.pad(jnp.concatenate([b_router_group[0], b_router_expert[0]]),
                  (0, LANES - n_groups - n_exp)).reshape(1, LANES)

    h_p = jnp.concatenate([jnp.broadcast_to(meta_tokens[None], (bp, N_META, d)), x_prompt], axis=1).reshape(npr, d)
    h_s = x_sample.reshape(ns, d)

    tm_p = _div_tile(npr, 1400, 16)
    tm_s = _div_tile(ns, 1400, 16)
    mix = functools.partial(_mixer_inputs, norm_g=norm_mix_g[0], w_in_bf=w_in_bf, wf=wf, bf=bf, q_g=q_norm_g[0],
                            k_g=k_norm_g[0], conv_dim=conv_dim, attn_w=attn_w, n_heads=n_heads)
    glu_p, q_p, k32_p, k16_p, v32_p, v16_p, logf_p = mix(h_p, tm=tm_p)
    glu_s, q_s, k32_s, k16_s, v32_s, v16_s, logf_s = mix(h_s, tm=tm_s)

    glu_p3 = glu_p.reshape(bp, t, conv_dim)
    glu_s3 = glu_s.reshape(bs, tq, conv_dim)
    hist_p = jnp.zeros((bp, hist_len + 2, conv_dim), F32)
    hist_s = jnp.pad(cache_conv[0], ((0, 0), (2, 0), (0, 0)))
    conv_args = (conv_w[0], conv_b[0], conv_ln_g[0], conv_ln_b[0])
    conv_p = _conv_module(glu_p3, hist_p, *conv_args)
    conv_s = _conv_module(glu_s3, hist_s, *conv_args)

    logf_p3 = logf_p.reshape(bp, t, n_heads)
    logf_s3 = logf_s.reshape(bs, tq, n_heads)
    cblk = 256
    pad_t = lambda a: jnp.pad(a, ((0, 0), (0, _cdiv(a.shape[1], cblk) * cblk - a.shape[1]), (0, 0)))
    cum_p = _cumsum_time(pad_t(logf_p3), cblk)[:, :t]
    lf_all = jnp.concatenate([cache_logf[0], logf_s3], axis=1)
    cum_s = _cumsum_time(pad_t(lf_all), cblk)[:, :past + tq]
    r3 = lambda a, b_, t_: a.reshape(b_, t_, attn_w)
    attn_p = _attn_prompt(r3(q_p, bp, t), r3(k16_p, bp, t), r3(v16_p, bp, t), cum_p, n_heads)
    attn_s = _attn_sample(r3(q_s, bs, tq), cache_k[0].reshape(bs, past, attn_w), cache_v[0].reshape(bs, past, attn_w),
                          r3(k16_s, bs, tq), r3(v16_s, bs, tq), cum_s, n_heads)

    tn_o = 512
    h2 = _outproj(attn_p.reshape(npr, attn_w), conv_p.reshape(npr, conv_dim), w_out_bf, h_p, n_all, 0, tm_p, tn_o)
    tm_s2 = _div_tile(_gcd(npr, ns), 256, 16)
    h2 = _outproj(attn_s.reshape(ns, attn_w), conv_s.reshape(ns, conv_dim), w_out_bf, h_s, n_all, npr, tm_s2, tn_o,
                  prev=h2)

    tm_r = _div_tile(n_all, 128, 8)
    eid, gate, xp = _router(h2, norm_ffn_g[0], w_r, b_r, n_groups, per_group, tm_r)
    tabs = _dispatch_tables(eid[:, :TOP_K], n_exp)
    x_pad = _gather_rows(xp, tabs)
    y_pad = _moe_mlp(x_pad, w_gate[0], w_up[0], w_down[0], tabs)

    tm_cp = _div_tile(seq, 128, 8)
    tiles_b = seq // tm_cp
    i_p = jnp.arange(bp * tiles_b, dtype=jnp.int32)
    row0_p = (i_p // tiles_b) * t + N_META + (i_p % tiles_b) * tm_cp
    tm_cs = _div_tile(ns, 128, 8)
    row0_s = npr + jnp.arange(ns // tm_cs, dtype=jnp.int32) * tm_cs
    y_p = _combine(y_pad, h2, gate, tabs['dest'], row0_p, tm_cp).reshape(bp, seq, d)
    y_s = _combine(y_pad, h2, gate, tabs['dest'], row0_s, tm_cs).reshape(bs, tq, d)

    st = lambda a, b_, t_: a.reshape(1, b_, t_, n_heads, hd)
    return (y_p, y_s,
            st(k32_p, bp, t), st(v32_p, bp, t), logf_p3[None], glu_p3[:, t - hist_len:][None],
            st(k32_s, bs, tq), st(v32_s, bs, tq), logf_s3[None],
            jnp.concatenate([cache_conv[0], glu_s3], axis=1)[:, -hist_len:][None])
```

```python
import functools

import jax
import jax.numpy as jnp
from jax import lax
from jax.experimental import pallas as pl
from jax.experimental.pallas import tpu as pltpu

F32 = jnp.float32
BF16 = jnp.bfloat16
U32 = jnp.uint32

N_META = 16
CONV_GROUPS = 16
TOP_K = 2
EPS = 1e-6
NEG_INF = -1e30

LANES = 128
SUBLANES = 8
ATTN_BLOCK = 512
MOE_SUB = 256
MOE_SUBS_PER_CHUNK = 3
DMA_UNROLL = 8
GATHER_GROUP = 32
LOG2E = 1.4426950408889634
BIAS_PARTS = 3
VMEM_LIMIT = 56 * 1024 * 1024


def _cdiv(a, b):
    return -(-a // b)


def _gcd(a, b):
    while b:
        a, b = b, a % b
    return a


def _div_tile(n, target, mult):
    best = None
    for d in range(mult, min(n, target) + 1, mult):
        if n % d == 0:
            best = d
    assert best is not None, (n, target, mult)
    return best


def _params(*sem):
    return pltpu.CompilerParams(dimension_semantics=sem, vmem_limit_bytes=VMEM_LIMIT)


def _rmsnorm_kernel(x_ref, g_ref, o_ref):
    x = x_ref[...]
    ms = jnp.mean(x * x, axis=-1, keepdims=True)
    o_ref[...] = (x * lax.rsqrt(ms + EPS) * g_ref[...]).astype(o_ref.dtype)


def _rmsnorm(x, g, tm):
    n, d = x.shape
    return pl.pallas_call(
        _rmsnorm_kernel,
        out_shape=jax.ShapeDtypeStruct((n, d), BF16),
        grid=(n // tm,),
        in_specs=[pl.BlockSpec((tm, d), lambda i: (i, 0)), pl.BlockSpec((1, d), lambda i: (0, 0))],
        out_specs=pl.BlockSpec((tm, d), lambda i: (i, 0)),
        compiler_params=_params("parallel"),
        name="rmsnorm",
    )(x, g.reshape(1, d))


def _glu_kernel(x_ref, wa_ref, wb_ref, o_ref):
    x = x_ref[...]
    a = jnp.dot(x, wa_ref[...], preferred_element_type=F32)
    b = jnp.dot(x, wb_ref[...], preferred_element_type=F32)
    o_ref[...] = a * jax.nn.sigmoid(b)


def _proj_glu(xn, w, conv_dim, tm, tn):
    n, d = xn.shape
    nb = conv_dim // tn
    return pl.pallas_call(
        _glu_kernel,
        out_shape=jax.ShapeDtypeStruct((n, conv_dim), F32),
        grid=(n // tm, nb),
        in_specs=[pl.BlockSpec((tm, d), lambda i, j: (i, 0)),
                  pl.BlockSpec((d, tn), lambda i, j: (0, j)),
                  pl.BlockSpec((d, tn), lambda i, j: (0, nb + j))],
        out_specs=pl.BlockSpec((tm, tn), lambda i, j: (i, j)),
        compiler_params=_params("parallel", "arbitrary"),
        name="proj_glu",
    )(xn, w, w)


def _headnorm_kernel(x_ref, w_ref, g_ref, *o_refs, scale, hd):
    z = jnp.dot(x_ref[...], w_ref[...], preferred_element_type=F32)
    g = g_ref[...]
    for h in range(z.shape[1] // hd):
        zh = z[:, h * hd:(h + 1) * hd]
        y = zh * lax.rsqrt(jnp.mean(zh * zh, axis=-1, keepdims=True) + EPS) * g
        if scale is not None:
            y = y * scale
        for o_ref in o_refs:
            o_ref[:, h * hd:(h + 1) * hd] = y.astype(o_ref.dtype)


def _proj_headnorm(xn, w, g, col0, width, hd, scale, out_dtypes, tm, tn):
    n, d = xn.shape
    nb = width // tn
    b0 = col0 // tn
    outs = pl.pallas_call(
        functools.partial(_headnorm_kernel, scale=scale, hd=hd),
        out_shape=[jax.ShapeDtypeStruct((n, width), dt) for dt in out_dtypes],
        grid=(n // tm, nb),
        in_specs=[pl.BlockSpec((tm, d), lambda i, j: (i, 0)),
                  pl.BlockSpec((d, tn), lambda i, j: (0, b0 + j)),
                  pl.BlockSpec((1, hd), lambda i, j: (0, 0))],
        out_specs=[pl.BlockSpec((tm, tn), lambda i, j: (i, j)) for _ in out_dtypes],
        compiler_params=_params("parallel", "arbitrary"),
        name="proj_headnorm",
    )(xn, w, g.reshape(1, hd))
    return outs


def _plain_kernel(x_ref, w_ref, *o_refs):
    z = jnp.dot(x_ref[...], w_ref[...], preferred_element_type=F32)
    for o_ref in o_refs:
        o_ref[...] = z.astype(o_ref.dtype)


def _proj_plain(xn, w, col0, width, out_dtypes, tm, tn):
    n, d = xn.shape
    nb = width // tn
    b0 = col0 // tn
    return pl.pallas_call(
        _plain_kernel,
        out_shape=[jax.ShapeDtypeStruct((n, width), dt) for dt in out_dtypes],
        grid=(n // tm, nb),
        in_specs=[pl.BlockSpec((tm, d), lambda i, j: (i, 0)),
                  pl.BlockSpec((d, tn), lambda i, j: (0, b0 + j))],
        out_specs=[pl.BlockSpec((tm, tn), lambda i, j: (i, j)) for _ in out_dtypes],
        compiler_params=_params("parallel", "arbitrary"),
        name="proj_plain",
    )(xn, w)


def _logf_kernel(x_ref, w_ref, b_ref, o_ref):
    z = jnp.dot(x_ref[...], w_ref[...], preferred_element_type=F32) + b_ref[...]
    o_ref[...] = jnp.minimum(z, 0.0) - jnp.log1p(jnp.exp(-jnp.abs(z)))


def _proj_logf(xn, wf, bf, tm):
    n, d = xn.shape
    return pl.pallas_call(
        _logf_kernel,
        out_shape=jax.ShapeDtypeStruct((n, LANES), F32),
        grid=(n // tm,),
        in_specs=[pl.BlockSpec((tm, d), lambda i: (i, 0)),
                  pl.BlockSpec((d, LANES), lambda i: (0, 0)),
                  pl.BlockSpec((1, LANES), lambda i: (0, 0))],
        out_specs=pl.BlockSpec((tm, LANES), lambda i: (i, 0)),
        compiler_params=_params("parallel"),
        name="proj_logf",
    )(xn, wf, bf)


def _conv_kernel(glu_ref, hist_ref, cw_ref, cb_ref, lg_ref, lb_ref, o_ref, full_sc, *, t, tt, width):
    hp = hist_ref.shape[1]
    lanes = glu_ref.shape[2]
    full_sc[0:hp, :] = hist_ref[0]
    full_sc[hp:hp + t, :] = glu_ref[0]
    full_sc[hp + t:hp + t + SUBLANES, :] = jnp.zeros((SUBLANES, lanes), F32)

    def chunk(ci, carry):
        t0 = pl.multiple_of(ci * tt, 16)
        for lt in range(lanes // LANES):
            ls = slice(lt * LANES, (lt + 1) * LANES)
            win = full_sc[pl.ds(t0, tt + hp + SUBLANES), ls]
            y = None
            for r in range(SUBLANES):
                part = None
                for a in range(_cdiv(width + 2, SUBLANES)):
                    j = SUBLANES * a + r
                    if not 2 <= j < width + 2:
                        continue
                    term = win[SUBLANES * a:SUBLANES * a + tt + SUBLANES, :] * cw_ref[j - 2:j - 1, ls]
                    part = term if part is None else part + term
                if part is None:
                    continue
                shifted = part[r:r + tt, :]
                y = shifted if y is None else y + shifted
            y = y + cb_ref[:, ls]
            mu = jnp.mean(y, axis=-1, keepdims=True)
            dlt = y - mu
            var = jnp.mean(dlt * dlt, axis=-1, keepdims=True)
            yn = dlt * lax.rsqrt(var + EPS) * lg_ref[:, ls] + lb_ref[:, ls]
            o_ref[0, pl.ds(t0, tt), ls] = (yn * jax.nn.sigmoid(yn)).astype(o_ref.dtype)
        return carry

    lax.fori_loop(0, t // tt, chunk, 0)


def _conv_module(glu, hist, conv_w, conv_b, ln_g, ln_b):
    b, t, c = glu.shape
    width = conv_w.shape[0]
    hp = hist.shape[1]
    assert hp == 32 and width + 1 == hp and c // CONV_GROUPS == LANES
    tt = _div_tile(t, 64, 16)
    tc = 2 * LANES
    cw = jnp.concatenate([conv_w, jnp.zeros((hp - width, c), F32)], axis=0)
    vec = lambda v: v.reshape(1, c)
    return pl.pallas_call(
        functools.partial(_conv_kernel, t=t, tt=tt, width=width),
        out_shape=jax.ShapeDtypeStruct((b, t, c), BF16),
        grid=(b, c // tc),
        in_specs=[pl.BlockSpec((1, t, tc), lambda i, j: (i, 0, j)),
                  pl.BlockSpec((1, hp, tc), lambda i, j: (i, 0, j)),
                  pl.BlockSpec((hp, tc), lambda i, j: (0, j)),
                  pl.BlockSpec((1, tc), lambda i, j: (0, j)),
                  pl.BlockSpec((1, tc), lambda i, j: (0, j)),
                  pl.BlockSpec((1, tc), lambda i, j: (0, j))],
        out_specs=pl.BlockSpec((1, t, tc), lambda i, j: (i, 0, j)),
        scratch_shapes=[pltpu.VMEM((hp + t + SUBLANES, tc), F32)],
        compiler_params=_params("parallel", "parallel"),
        name="conv_ln_swish",
    )(glu, hist, cw, vec(conv_b), vec(ln_g), vec(ln_b))


def _cumsum_kernel(x_ref, o_ref, *, nblk, blk):
    r = lax.broadcasted_iota(jnp.int32, (blk, blk), 0)
    c = lax.broadcasted_iota(jnp.int32, (blk, blk), 1)
    tri = (c <= r).astype(F32)
    carry = jnp.zeros((1, x_ref.shape[2]), F32)
    for i in range(nblk):
        x = x_ref[0, i * blk:(i + 1) * blk, :]
        o_ref[0, i * blk:(i + 1) * blk, :] = jnp.dot(
            tri, x, precision=lax.Precision.HIGHEST, preferred_element_type=F32) + carry
        carry = carry + jnp.sum(x, axis=0, keepdims=True)


def _cumsum_time(x, blk):
    b, t, h = x.shape
    return pl.pallas_call(
        functools.partial(_cumsum_kernel, nblk=t // blk, blk=blk),
        out_shape=jax.ShapeDtypeStruct((b, t, h), F32),
        grid=(b,),
        in_specs=[pl.BlockSpec((1, t, h), lambda i: (i, 0, 0))],
        out_specs=pl.BlockSpec((1, t, h), lambda i: (i, 0, 0)),
        compiler_params=_params("parallel"),
        name="logf_cumsum",
    )(x)


def _attn_update(q, k, v, cq, ck, mask, m, l, acc):
    s = lax.dot_general(q, k, (((1,), (1,)), ((), ())), preferred_element_type=F32)
    s = s - ck
    if mask is not None:
        s = jnp.where(mask, s, NEG_INF)
    m_new = jnp.maximum(m, jnp.max(s, axis=-1, keepdims=True) + cq)
    alpha = jnp.exp(m - m_new)
    p = jnp.exp(s + (cq - m_new))
    l_new = alpha * l + jnp.sum(p, axis=-1, keepdims=True)
    acc_new = alpha * acc + jnp.dot(p.astype(BF16), v, preferred_element_type=F32)
    return m_new, l_new, acc_new


def _softmax_init(tq, d):
    return (jnp.full((tq, 1), NEG_INF, F32), jnp.zeros((tq, 1), F32), jnp.zeros((tq, d), F32))


def _head_column(c, h):
    sel = lax.broadcasted_iota(jnp.int32, c.shape, 1) == h
    return jnp.sum(jnp.where(sel, c, 0.0), axis=-1, keepdims=True)


def _causal_mask(n):
    r = lax.broadcasted_iota(jnp.int32, (n, n), 0)
    c = lax.broadcasted_iota(jnp.int32, (n, n), 1)
    return c <= r


def _attn_prompt_kernel(q_ref, k_ref, v_ref, cum_ref, o_ref, qa_sc, ka_sc, s_sc, d_sc, p_sc, pd_sc, l_sc,
                        *, t, bq, rs):
    h = pl.program_id(1)
    d = q_ref.shape[2]
    nfull = t // bq
    tail = t - nfull * bq
    nt = (((1,), (1,)), ((), ()))

    lane = lax.broadcasted_iota(jnp.int32, (t, d), 1)
    rem = _head_column(cum_ref[0], h) * LOG2E
    aug = jnp.zeros((t, d), F32)
    for j in range(BIAS_PARTS):
        part = rem.astype(BF16).astype(F32)
        aug = jnp.where(lane == j, part, aug)
        rem = rem - part
    ka_sc[:, 0:d] = k_ref[0]
    ka_sc[:, d:2 * d] = aug.astype(BF16)
    qa_sc[:, 0:d] = q_ref[0]
    qa_sc[:, d:2 * d] = jnp.where(lane < BIAS_PARTS, -1.0, 0.0).astype(BF16)

    blocks = [(i * bq, bq) for i in range(nfull)] + ([(nfull * bq, tail)] if tail else [])
    for q0, nq in blocks:
        q = qa_sc[q0:q0 + nq, :]
        if q0:
            s_sc[0:nq, 0:q0] = lax.dot_general(q, ka_sc[0:q0, :], nt, preferred_element_type=F32)
        d_sc[0:nq, 0:nq] = lax.dot_general(q, ka_sc[q0:q0 + nq, :], nt, preferred_element_type=F32)
        rows = min(rs, nq)

        def softmax_rows(ri, carry, q0=q0, nq=nq, rows=rows):
            r0 = pl.multiple_of(ri * rows, rows)
            cq = _head_column(cum_ref[0, pl.ds(q0 + r0, rows), :], h) * LOG2E
            rr = lax.broadcasted_iota(jnp.int32, (rows, nq), 0) + r0
            cc = lax.broadcasted_iota(jnp.int32, (rows, nq), 1)
            sd = jnp.where(cc <= rr, d_sc[pl.ds(r0, rows), 0:nq], NEG_INF)
            mx = jnp.max(sd, axis=-1, keepdims=True)
            if q0:
                sp = s_sc[pl.ds(r0, rows), 0:q0]
                mx = jnp.maximum(mx, jnp.max(sp, axis=-1, keepdims=True))
            shift = cq - (mx + cq)
            pd = jnp.exp2(sd + shift)
            l = jnp.sum(pd, axis=-1, keepdims=True)
            pd_sc[pl.ds(r0, rows), 0:nq] = pd.astype(BF16)
            if q0:
                pp = jnp.exp2(sp + shift)
                l = l + jnp.sum(pp, axis=-1, keepdims=True)
                p_sc[pl.ds(r0, rows), 0:q0] = pp.astype(BF16)
            l_sc[pl.ds(r0, rows), :] = l
            return carry

        lax.fori_loop(0, nq // rows, softmax_rows, 0)
        o = jnp.dot(pd_sc[0:nq, 0:nq], v_ref[0, q0:q0 + nq, :], preferred_element_type=F32)
        if q0:
            o = o + jnp.dot(p_sc[0:nq, 0:q0], v_ref[0, 0:q0, :], preferred_element_type=F32)
        o_ref[0, q0:q0 + nq, :] = (o / l_sc[0:nq, :]).astype(o_ref.dtype)


def _blocked_rows(cum, blk):
    b, t, h = cum.shape
    nblk = _cdiv(t, blk)
    c = jnp.pad(cum, ((0, 0), (0, nblk * blk - t), (0, 0)))
    return c.transpose(0, 2, 1).reshape(b, h, nblk, 1, blk)


def _attn_prompt(q, k, v, cum, n_heads):
    b, t, w = q.shape
    d = w // n_heads
    bq = ATTN_BLOCK
    nfull = t // bq
    assert nfull >= 1 and (t % bq) % 16 == 0
    blk = lambda: pl.BlockSpec((1, t, d), lambda i, j: (i, 0, j))
    return pl.pallas_call(
        functools.partial(_attn_prompt_kernel, t=t, bq=bq, rs=128),
        out_shape=jax.ShapeDtypeStruct((b, t, w), BF16),
        grid=(b, n_heads),
        in_specs=[blk(), blk(), blk(),
                  pl.BlockSpec((1, t, n_heads), lambda i, j: (i, 0, 0))],
        out_specs=blk(),
        scratch_shapes=[pltpu.VMEM((t, 2 * d), BF16), pltpu.VMEM((t, 2 * d), BF16),
                        pltpu.VMEM((bq, nfull * bq), F32), pltpu.VMEM((bq, bq), F32),
                        pltpu.VMEM((bq, nfull * bq), BF16), pltpu.VMEM((bq, bq), BF16),
                        pltpu.VMEM((bq, 1), F32)],
        compiler_params=_params("parallel", "parallel"),
        name="fox_attn_prompt",
    )(q, k, v, cum)


def _attn_sample_kernel(q_ref, ck_ref, cv_ref, k_ref, v_ref, cum_ref, cumt_ref, o_ref, *, past, bk):
    h = pl.program_id(1)
    tq, d = q_ref.shape[1], q_ref.shape[2]
    q = q_ref[0]
    cq = _head_column(cum_ref[0], h)
    carry = _softmax_init(tq, d)
    for kj in range(past // bk):
        kb = ck_ref[0, kj * bk:(kj + 1) * bk, :].astype(BF16)
        vb = cv_ref[0, kj * bk:(kj + 1) * bk, :].astype(BF16)
        carry = _attn_update(q, kb, vb, cq, cumt_ref[0, 0, kj], None, *carry)
    carry = _attn_update(q, k_ref[0], v_ref[0], cq, cumt_ref[0, 0, past // bk][:, :tq], _causal_mask(tq), *carry)
    _, l, acc = carry
    o_ref[0] = (acc / l).astype(o_ref.dtype)


def _attn_sample(q, cache_k, cache_v, k, v, cum, n_heads):
    b, tq, w = q.shape
    d = w // n_heads
    past = cache_k.shape[1]
    bk = _div_tile(past, ATTN_BLOCK, LANES)
    assert tq <= bk
    cumt = _blocked_rows(cum, bk)
    nblk = cumt.shape[2]
    cum_q = cum[:, past:]
    new = lambda: pl.BlockSpec((1, tq, d), lambda i, j: (i, 0, j))
    old = lambda: pl.BlockSpec((1, past, d), lambda i, j: (i, 0, j))
    return pl.pallas_call(
        functools.partial(_attn_sample_kernel, past=past, bk=bk),
        out_shape=jax.ShapeDtypeStruct((b, tq, w), BF16),
        grid=(b, n_heads),
        in_specs=[new(), old(), old(), new(), new(),
                  pl.BlockSpec((1, tq, n_heads), lambda i, j: (i, 0, 0)),
                  pl.BlockSpec((1, 1, nblk, 1, bk), lambda i, j: (i, j, 0, 0, 0))],
        out_specs=new(),
        compiler_params=_params("parallel", "parallel"),
        name="fox_attn_sample",
    )(q, cache_k, cache_v, k, v, cum_q, cumt)


def _outproj_kernel(ap_ref, cp_ref, hp_ref, as_ref, cs_ref, hs_ref, wa_ref, wc_ref, o_ref, *, n_p):
    i = pl.program_id(0)

    def project(a_ref, c_ref, h_ref):
        return (h_ref[...]
                + jnp.dot(a_ref[...], wa_ref[...], preferred_element_type=F32)
                + jnp.dot(c_ref[...], wc_ref[...], preferred_element_type=F32))

    @pl.when(i < n_p)
    def _():
        o_ref[...] = project(ap_ref, cp_ref, hp_ref)

    @pl.when(i == n_p)
    def _():
        o_ref[0:hs_ref.shape[0], :] = project(as_ref, cs_ref, hs_ref)


def _outproj(attn_p, conv_p, h_p, attn_s, conv_s, h_s, w_out, tm, tn):
    npr, wa = attn_p.shape
    ns = attn_s.shape[0]
    wc = conv_p.shape[1]
    d = w_out.shape[1]
    assert wa == wc and npr % tm == 0 and ns <= tm
    n_p = npr // tm
    last = lambda i: jnp.minimum(i, n_p - 1)
    return pl.pallas_call(
        functools.partial(_outproj_kernel, n_p=n_p),
        out_shape=jax.ShapeDtypeStruct(((n_p + 1) * tm, d), F32),
        grid=(n_p + 1, d // tn),
        in_specs=[pl.BlockSpec((tm, wa), lambda i, j: (last(i), 0)),
                  pl.BlockSpec((tm, wc), lambda i, j: (last(i), 0)),
                  pl.BlockSpec((tm, tn), lambda i, j: (last(i), j)),
                  pl.BlockSpec((ns, wa), lambda i, j: (0, 0)),
                  pl.BlockSpec((ns, wc), lambda i, j: (0, 0)),
                  pl.BlockSpec((ns, tn), lambda i, j: (0, j)),
                  pl.BlockSpec((wa, tn), lambda i, j: (0, j)),
                  pl.BlockSpec((wc, tn), lambda i, j: (1, j))],
        out_specs=pl.BlockSpec((tm, tn), lambda i, j: (i, j)),
        compiler_params=_params("parallel", "arbitrary"),
        name="outproj_residual",
    )(attn_p, conv_p, h_p, attn_s, conv_s, h_s, w_out, w_out)


def _router_kernel(h_ref, g_ref, w_ref, b_ref, eid_ref, gate_ref, xp_ref, *, n_groups, per_group):
    x = h_ref[...]
    xn = (x * lax.rsqrt(jnp.mean(x * x, axis=-1, keepdims=True) + EPS) * g_ref[...]).astype(BF16)
    half = x.shape[1] // 2
    bits = pltpu.bitcast(xn.astype(F32), U32)
    xp_ref[...] = bits[:, :half] | (bits[:, half:] >> 16)

    lg = jnp.dot(xn, w_ref[...], preferred_element_type=F32) + b_ref[...]
    lane = lax.broadcasted_iota(jnp.int32, lg.shape, 1)
    ninf = -jnp.inf

    def top1(vals):
        mx = jnp.max(vals, axis=-1, keepdims=True)
        idx = jnp.min(jnp.where(vals == mx, lane, LANES), axis=-1, keepdims=True)
        return mx, idx

    gl = jnp.where(lane < n_groups, lg, ninf)
    gmax, gidx = top1(gl)
    p_g = 1.0 / jnp.sum(jnp.exp(gl - gmax), axis=-1, keepdims=True)
    lo = n_groups + per_group * gidx
    el = jnp.where((lane >= lo) & (lane < lo + per_group), lg, ninf)
    m1, i1 = top1(el)
    m2, i2 = top1(jnp.where(lane == i1, ninf, el))
    e21 = jnp.exp(m2 - m1)
    g1 = p_g / (1.0 + e21)
    g2 = g1 * e21
    eid_ref[...] = jnp.where(lane == 0, i1 - n_groups, jnp.where(lane == 1, i2 - n_groups, 0))
    gate_ref[...] = jnp.where(lane == 0, g1, jnp.where(lane == 1, g2, 0.0))


def _router(h, n, g, w_r, b_r, n_groups, per_group, tm):
    d = h.shape[1]
    row = lambda w: pl.BlockSpec((tm, w), lambda i: (i, 0))
    return pl.pallas_call(
        functools.partial(_router_kernel, n_groups=n_groups, per_group=per_group),
        out_shape=[jax.ShapeDtypeStruct((n, LANES), jnp.int32), jax.ShapeDtypeStruct((n, LANES), F32),
                   jax.ShapeDtypeStruct((n, d // 2), U32)],
        grid=(n // tm,),
        in_specs=[row(d),
                  pl.BlockSpec((1, d), lambda i: (0, 0)),
                  pl.BlockSpec((d, LANES), lambda i: (0, 0)),
                  pl.BlockSpec((1, LANES), lambda i: (0, 0))],
        out_specs=[row(LANES), row(LANES), row(d // 2)],
        compiler_params=_params("parallel"),
        name="router",
    )(h, g.reshape(1, d), w_r, b_r)


def _dispatch_tables(e_id, n_exp):
    n, k = e_id.shape
    m = n * k
    sb = MOE_SUB
    rc = sb * MOE_SUBS_PER_CHUNK
    n_chunks = _cdiv(m, rc) + n_exp
    n_sub_max = _cdiv(m, sb) + n_exp
    e_flat = e_id.reshape(m)
    onehot = (e_flat[:, None] == jnp.arange(n_exp, dtype=jnp.int32)[None, :]).astype(jnp.int32)
    csum = jnp.cumsum(onehot, axis=0)
    counts = csum[-1]
    rank = jnp.take_along_axis(csum, e_flat[:, None], axis=1)[:, 0] - 1
    cpe = (counts + rc - 1) // rc
    chunk_end = jnp.cumsum(cpe)
    chunk_start = chunk_end - cpe
    n_used = chunk_end[-1]
    dest = (chunk_start[e_flat] * rc + rank).astype(jnp.int32)
    c_ar = jnp.arange(n_chunks, dtype=jnp.int32)
    used = c_ar < n_used
    c_cl = jnp.minimum(c_ar, n_used - 1)
    chunk_e = jnp.minimum(jnp.searchsorted(chunk_end, c_cl, side='right'), n_exp - 1).astype(jnp.int32)
    rows = jnp.clip(counts[chunk_e] - (c_cl - chunk_start[chunk_e]) * rc, 0, rc)
    nsub = jnp.where(used, (rows + sb - 1) // sb, 0).astype(jnp.int32)
    in_blk = c_cl.astype(jnp.int32)
    out_blk = jnp.where(used, c_ar, n_chunks).astype(jnp.int32)
    tok = (jnp.arange(m, dtype=jnp.int32) // k)
    src = jnp.zeros((n_chunks * rc,), jnp.int32).at[dest].set(tok)
    sub_active = (jnp.arange(MOE_SUBS_PER_CHUNK, dtype=jnp.int32)[None, :] < nsub[:, None]).reshape(-1)
    n_act = jnp.sum(sub_active.astype(jnp.int32))
    sub_ids = jnp.nonzero(sub_active, size=n_sub_max, fill_value=0)[0].astype(jnp.int32)
    sub_ids = jnp.where(jnp.arange(n_sub_max) < n_act, sub_ids, sub_ids[n_act - 1])
    src_sub = src.reshape(n_chunks * MOE_SUBS_PER_CHUNK, sb)[sub_ids].reshape(-1)
    rows_sub = jnp.clip(rows[:, None] - jnp.arange(MOE_SUBS_PER_CHUNK, dtype=jnp.int32)[None, :] * sb, 0, sb)
    grp_sub = ((rows_sub.reshape(-1)[sub_ids] + GATHER_GROUP - 1) // GATHER_GROUP).astype(jnp.int32)
    return dict(dest=dest, chunk_e=chunk_e, nsub=nsub, in_blk=in_blk, out_blk=out_blk,
                sub_ids=sub_ids, src_sub=src_sub, grp_sub=grp_sub, n_act=n_act.reshape(1).astype(jnp.int32),
                n_chunks=n_chunks, n_sub_max=n_sub_max)


def _gather_kernel(src_ref, sub_ref, nact_ref, grp_ref, xp_hbm, o_ref, buf, sem, *, sb):
    i = pl.program_id(0)
    n_act = nact_ref[0]
    half = xp_hbm.shape[1]

    def issue(step, slot):
        def body(it, carry):
            for u in range(DMA_UNROLL):
                r = it * DMA_UNROLL + u
                tok = src_ref[step * sb + r]
                pltpu.make_async_copy(xp_hbm.at[pl.ds(tok, 1)], buf.at[slot, pl.ds(r, 1)], sem.at[slot]).start()
            return carry
        lax.fori_loop(0, grp_ref[step] * (GATHER_GROUP // DMA_UNROLL), body, 0)

    @pl.when(i == 0)
    def _():
        buf[...] = jnp.zeros(buf.shape, buf.dtype)
        issue(0, 0)

    @pl.when(i + 1 < n_act)
    def _():
        issue(i + 1, (i + 1) % 2)

    @pl.when(i < n_act)
    def _():
        slot = i % 2

        def wait_group(gi, carry):
            pltpu.make_async_copy(xp_hbm.at[pl.ds(0, GATHER_GROUP)], buf.at[slot, pl.ds(0, GATHER_GROUP)],
                                  sem.at[slot]).wait()
            return carry
        lax.fori_loop(0, grp_ref[i], wait_group, 0)
        rows = 32
        for c in range(sb // rows):
            u = buf[slot, c * rows:(c + 1) * rows, :]
            hi = pltpu.bitcast(u & jnp.uint32(0xFFFF0000), F32)
            lo = pltpu.bitcast(u << 16, F32)
            o_ref[c * rows:(c + 1) * rows, 0:half] = hi.astype(o_ref.dtype)
            o_ref[c * rows:(c + 1) * rows, half:2 * half] = lo.astype(o_ref.dtype)


def _gather_rows(xp, tabs):
    n, half = xp.shape
    sb = MOE_SUB
    rows = tabs['n_chunks'] * sb * MOE_SUBS_PER_CHUNK
    return pl.pallas_call(
        functools.partial(_gather_kernel, sb=sb),
        out_shape=jax.ShapeDtypeStruct((rows, 2 * half), BF16),
        grid_spec=pltpu.PrefetchScalarGridSpec(
            num_scalar_prefetch=4,
            grid=(tabs['n_sub_max'],),
            in_specs=[pl.BlockSpec(memory_space=pl.ANY)],
            out_specs=pl.BlockSpec((sb, 2 * half), lambda i, src, sub, nact, grp: (sub[i], 0)),
            scratch_shapes=[pltpu.VMEM((2, sb, half), U32), pltpu.SemaphoreType.DMA((2,))]),
        compiler_params=_params("arbitrary"),
        name="moe_gather",
    )(tabs['src_sub'], tabs['sub_ids'], tabs['n_act'], tabs['grp_sub'], xp)


def _moe_kernel(ce_ref, nsub_ref, ib_ref, ob_ref, x_ref, wg_ref, wu_ref, wd_ref, o_ref, hid_sc, *, sb, n_f, tf):
    c = pl.program_id(0)
    s = pl.program_id(1)
    nsub = nsub_ref[c]

    for r in range(1, MOE_SUBS_PER_CHUNK + 1):
        rows = r * sb

        @pl.when((nsub == r) & (s < n_f))
        def _():
            x = x_ref[0:rows, :]
            g = jnp.dot(x, wg_ref[0].astype(BF16), preferred_element_type=F32)
            u = jnp.dot(x, wu_ref[0].astype(BF16), preferred_element_type=F32)
            hid = (g * jax.nn.sigmoid(g) * u).astype(BF16)
            for f in range(n_f):
                @pl.when(s == f)
                def _():
                    hid_sc[0:rows, f * tf:(f + 1) * tf] = hid

        @pl.when((nsub == r) & (s >= n_f))
        def _():
            o_ref[0:rows, :] = jnp.dot(hid_sc[0:rows, :], wd_ref[0].astype(BF16), preferred_element_type=F32)


def _moe_mlp(x_pad, w_gate, w_up, w_down, tabs):
    n_exp, d, ff = w_gate.shape
    sb = MOE_SUB
    rc = sb * MOE_SUBS_PER_CHUNK
    tf = 256
    td = 1024
    n_f = ff // tf
    n_d = d // td
    n_chunks = tabs['n_chunks']
    f_idx = lambda c, s, ns: jnp.where(ns[c] > 0, jnp.minimum(s, n_f - 1), n_f - 1)
    d_idx = lambda c, s, ns: jnp.where(ns[c] > 0, jnp.maximum(s - n_f, 0), n_d - 1)
    wmap = lambda c, s, ce, ns, ib, ob: (ce[c], 0, f_idx(c, s, ns))
    return pl.pallas_call(
        functools.partial(_moe_kernel, sb=sb, n_f=n_f, tf=tf),
        out_shape=jax.ShapeDtypeStruct(((n_chunks + 1) * rc, d), F32),
        grid_spec=pltpu.PrefetchScalarGridSpec(
            num_scalar_prefetch=4,
            grid=(n_chunks, n_f + n_d),
            in_specs=[pl.BlockSpec((rc, d), lambda c, s, ce, ns, ib, ob: (ib[c], 0)),
                      pl.BlockSpec((1, d, tf), wmap),
                      pl.BlockSpec((1, d, tf), wmap),
                      pl.BlockSpec((1, ff, td), lambda c, s, ce, ns, ib, ob: (ce[c], 0, d_idx(c, s, ns)))],
            out_specs=pl.BlockSpec((rc, td), lambda c, s, ce, ns, ib, ob:
                                   (ob[c], jnp.where(ns[c] > 0, jnp.maximum(s - n_f, 0), 0))),
            scratch_shapes=[pltpu.VMEM((rc, ff), BF16)]),
        compiler_params=_params("arbitrary", "arbitrary"),
        name="moe_mlp",
    )(tabs['chunk_e'], tabs['nsub'], tabs['in_blk'], tabs['out_blk'], x_pad, w_gate, w_up, w_down)


def _combine_kernel(dest_ref, row0_ref, y_hbm, h_hbm, gate_hbm, o_ref, ybuf, hbuf, gbuf, sem, *, tm):
    i = pl.program_id(0)
    n = pl.num_programs(0)

    def contiguous(step, slot):
        r0 = pl.multiple_of(row0_ref[step], SUBLANES)
        return (pltpu.make_async_copy(h_hbm.at[pl.ds(r0, tm)], hbuf.at[slot], sem.at[slot]),
                pltpu.make_async_copy(gate_hbm.at[pl.ds(r0, tm)], gbuf.at[slot], sem.at[slot]))

    def issue(step, slot):
        for cp in contiguous(step, slot):
            cp.start()
        base = row0_ref[step] * TOP_K

        def body(it, carry):
            for u in range(DMA_UNROLL // TOP_K):
                r = it * (DMA_UNROLL // TOP_K) + u
                for k in range(TOP_K):
                    row = dest_ref[base + r * TOP_K + k]
                    pltpu.make_async_copy(y_hbm.at[pl.ds(row, 1)], ybuf.at[slot, k, pl.ds(r, 1)],
                                          sem.at[slot]).start()
            return carry
        lax.fori_loop(0, tm * TOP_K // DMA_UNROLL, body, 0)

    @pl.when(i == 0)
    def _():
        issue(0, 0)

    @pl.when(i + 1 < n)
    def _():
        issue(i + 1, (i + 1) % 2)

    slot = i % 2
    for cp in contiguous(i, slot):
        cp.wait()
    for k in range(TOP_K):
        pltpu.make_async_copy(y_hbm.at[pl.ds(0, tm)], ybuf.at[slot, k], sem.at[slot]).wait()
    gate = gbuf[slot]
    out = hbuf[slot]
    for k in range(TOP_K):
        out = out + gate[:, k:k + 1] * ybuf[slot, k]
    o_ref[...] = out


def _combine(y_pad, h, gate, dest, row0, tm):
    d = h.shape[1]
    n_tiles = row0.shape[0]
    return pl.pallas_call(
        functools.partial(_combine_kernel, tm=tm),
        out_shape=jax.ShapeDtypeStruct((n_tiles * tm, d), F32),
        grid_spec=pltpu.PrefetchScalarGridSpec(
            num_scalar_prefetch=2,
            grid=(n_tiles,),
            in_specs=[pl.BlockSpec(memory_space=pl.ANY), pl.BlockSpec(memory_space=pl.ANY),
                      pl.BlockSpec(memory_space=pl.ANY)],
            out_specs=pl.BlockSpec((tm, d), lambda i, dest, row0: (i, 0)),
            scratch_shapes=[pltpu.VMEM((2, TOP_K, tm, d), F32), pltpu.VMEM((2, tm, d), F32),
                            pltpu.VMEM((2, tm, LANES), F32), pltpu.SemaphoreType.DMA((2,))]),
        compiler_params=_params("arbitrary"),
        name="moe_combine",
    )(dest, row0, y_pad, h, gate)


def _mixer_inputs(h, norm_g, w_in_bf, wf, bf, q_g, k_g, conv_dim, attn_w, n_heads, tm, q_scale):
    hd = attn_w // n_heads
    tn = 512
    xn = _rmsnorm(h, norm_g, _div_tile(h.shape[0], 256, 8))
    glu = _proj_glu(xn, w_in_bf, conv_dim, tm, tn // 2)
    c1 = 2 * conv_dim
    (q,) = _proj_headnorm(xn, w_in_bf, q_g, c1, attn_w, hd, q_scale, (BF16,), tm, tn)
    k32, k16 = _proj_headnorm(xn, w_in_bf, k_g, c1 + attn_w, attn_w, hd, None, (F32, BF16), tm, tn)
    v32, v16 = _proj_plain(xn, w_in_bf, c1 + 2 * attn_w, attn_w, (F32, BF16), tm, tn)
    logf = _proj_logf(xn, wf, bf, tm)[:, :n_heads]
    return glu, q, k32, k16, v32, v16, logf


def kernel(x_prompt, x_sample, cache_k, cache_v, cache_logf, cache_conv, meta_tokens, norm_mix_g, w_in, b_forget,
           q_norm_g, k_norm_g, conv_w, conv_b, conv_ln_g, conv_ln_b, w_out, norm_ffn_g, w_router_group,
           b_router_group, w_router_expert, b_router_expert, w_gate, w_up, w_down):
    depth = w_in.shape[0]
    assert depth == 1
    bp, seq, d = x_prompt.shape
    bs, tq, _ = x_sample.shape
    n_heads = cache_k.shape[3]
    hd = cache_k.shape[4]
    attn_w = n_heads * hd
    conv_dim = conv_w.shape[2]
    hist_len = conv_w.shape[1] - 1
    past = cache_k.shape[2]
    n_groups = w_router_group.shape[2]
    n_exp = w_router_expert.shape[2]
    per_group = n_exp // n_groups
    assert n_groups + n_exp <= LANES and 2 * conv_dim % 512 == 0 and attn_w % 512 == 0
    t = N_META + seq
    npr = bp * t
    ns = bs * tq
    n_all = npr + ns

    w_in_bf = w_in[0].astype(BF16)
    c4 = 2 * conv_dim + 3 * attn_w
    wf = jnp.pad(w_in[0][:, c4:], ((0, 0), (0, LANES - n_heads))).astype(BF16)
    bf = jnp.pad(b_forget[0], (0, LANES - n_heads)).reshape(1, LANES)
    w_out_bf = w_out[0].astype(BF16)
    w_r = jnp.pad(jnp.concatenate([w_router_group[0], w_router_expert[0]], axis=1),
                  ((0, 0), (0, LANES - n_groups - n_exp))).astype(BF16)
    b_r = jnp.pad(jnp.concatenate([b_router_group[0], b_router_expert[0]]),
                  (0, LANES - n_groups - n_exp)).reshape(1, LANES)

    h_p = jnp.concatenate([jnp.broadcast_to(meta_tokens[None], (bp, N_META, d)), x_prompt], axis=1).reshape(npr, d)
    h_s = x_sample.reshape(ns, d)

    tm_p = _div_tile(npr, 1400, 16)
    tm_s = _div_tile(ns, 1400, 16)
    mix = functools.partial(_mixer_inputs, norm_g=norm_mix_g[0], w_in_bf=w_in_bf, wf=wf, bf=bf, q_g=q_norm_g[0],
                            k_g=k_norm_g[0], conv_dim=conv_dim, attn_w=attn_w, n_heads=n_heads)
    glu_p, q_p, k32_p, k16_p, v32_p, v16_p, logf_p = mix(h_p, tm=tm_p, q_scale=hd ** -0.5 * LOG2E)
    glu_s, q_s, k32_s, k16_s, v32_s, v16_s, logf_s = mix(h_s, tm=tm_s, q_scale=hd ** -0.5)

    glu_p3 = glu_p.reshape(bp, t, conv_dim)
    glu_s3 = glu_s.reshape(bs, tq, conv_dim)
    hist_p = jnp.zeros((bp, hist_len + 2, conv_dim), F32)
    hist_s = jnp.pad(cache_conv[0], ((0, 0), (2, 0), (0, 0)))
    conv_args = (conv_w[0], conv_b[0], conv_ln_g[0], conv_ln_b[0])
    conv_p = _conv_module(glu_p3, hist_p, *conv_args)
    conv_s = _conv_module(glu_s3, hist_s, *conv_args)

    logf_p3 = logf_p.reshape(bp, t, n_heads)
    logf_s3 = logf_s.reshape(bs, tq, n_heads)
    cblk = 256
    pad_t = lambda a: jnp.pad(a, ((0, 0), (0, _cdiv(a.shape[1], cblk) * cblk - a.shape[1]), (0, 0)))
    cum_p = _cumsum_time(pad_t(logf_p3), cblk)[:, :t]
    lf_all = jnp.concatenate([cache_logf[0], logf_s3], axis=1)
    cum_s = _cumsum_time(pad_t(lf_all), cblk)[:, :past + tq]
    r3 = lambda a, b_, t_: a.reshape(b_, t_, attn_w)
    attn_p = _attn_prompt(r3(q_p, bp, t), r3(k16_p, bp, t), r3(v16_p, bp, t), cum_p, n_heads)
    attn_s = _attn_sample(r3(q_s, bs, tq), cache_k[0].reshape(bs, past, attn_w), cache_v[0].reshape(bs, past, attn_w),
                          r3(k16_s, bs, tq), r3(v16_s, bs, tq), cum_s, n_heads)

    tn_o = 512
    h2 = _outproj(attn_p.reshape(npr, attn_w), conv_p.reshape(npr, conv_dim), h_p,
                  attn_s.reshape(ns, attn_w), conv_s.reshape(ns, conv_dim), h_s, w_out_bf, tm_p, tn_o)

    tm_r = _div_tile(n_all, 128, 8)
    eid, gate, xp = _router(h2, n_all, norm_ffn_g[0], w_r, b_r, n_groups, per_group, tm_r)
    tabs = _dispatch_tables(eid[:, :TOP_K], n_exp)
    x_pad = _gather_rows(xp, tabs)
    y_pad = _moe_mlp(x_pad, w_gate[0], w_up[0], w_down[0], tabs)

    tm_cp = _div_tile(seq, 128, 8)
    tiles_b = seq // tm_cp
    i_p = jnp.arange(bp * tiles_b, dtype=jnp.int32)
    row0_p = (i_p // tiles_b) * t + N_META + (i_p % tiles_b) * tm_cp
    tm_cs = _div_tile(ns, 128, 8)
    row0_s = npr + jnp.arange(ns // tm_cs, dtype=jnp.int32) * tm_cs
    y_p = _combine(y_pad, h2, gate, tabs['dest'], row0_p, tm_cp).reshape(bp, seq, d)
    y_s = _combine(y_pad, h2, gate, tabs['dest'], row0_s, tm_cs).reshape(bs, tq, d)

    st = lambda a, b_, t_: a.reshape(1, b_, t_, n_heads, hd)
    return (y_p, y_s,
            st(k32_p, bp, t), st(v32_p, bp, t), logf_p3[None], glu_p3[:, t - hist_len:][None],
            st(k32_s, bs, tq), st(v32_s, bs, tq), logf_s3[None],
            jnp.concatenate([cache_conv[0], glu_s3], axis=1)[:, -hist_len:][None])
```

```python
import functools

import jax
import jax.numpy as jnp
from jax import lax
from jax.experimental import pallas as pl
from jax.experimental.pallas import tpu as pltpu

F32 = jnp.float32
BF16 = jnp.bfloat16
U32 = jnp.uint32

N_META = 16
CONV_GROUPS = 16
TOP_K = 2
EPS = 1e-6
NEG_INF = -1e30

LANES = 128
SUBLANES = 8
ATTN_BLOCK = 512
MOE_SUB = 128
MOE_SUBS_PER_CHUNK = 6
DMA_UNROLL = 8
GATHER_GROUP = 32
LOG2E = 1.4426950408889634
BIAS_PARTS = 3
VMEM_LIMIT = 56 * 1024 * 1024


def _cdiv(a, b):
    return -(-a // b)


def _gcd(a, b):
    while b:
        a, b = b, a % b
    return a


def _div_tile(n, target, mult):
    best = None
    for d in range(mult, min(n, target) + 1, mult):
        if n % d == 0:
            best = d
    assert best is not None, (n, target, mult)
    return best


def _params(*sem):
    return pltpu.CompilerParams(dimension_semantics=sem, vmem_limit_bytes=VMEM_LIMIT)


def _rmsnorm_kernel(x_ref, g_ref, o_ref):
    x = x_ref[...]
    ms = jnp.mean(x * x, axis=-1, keepdims=True)
    o_ref[...] = (x * lax.rsqrt(ms + EPS) * g_ref[...]).astype(o_ref.dtype)


def _rmsnorm(x, g, tm):
    n, d = x.shape
    return pl.pallas_call(
        _rmsnorm_kernel,
        out_shape=jax.ShapeDtypeStruct((n, d), BF16),
        grid=(n // tm,),
        in_specs=[pl.BlockSpec((tm, d), lambda i: (i, 0)), pl.BlockSpec((1, d), lambda i: (0, 0))],
        out_specs=pl.BlockSpec((tm, d), lambda i: (i, 0)),
        compiler_params=_params("parallel"),
        name="rmsnorm",
    )(x, g.reshape(1, d))


def _glu_kernel(x_ref, wa_ref, wb_ref, o_ref):
    x = x_ref[...]
    a = jnp.dot(x, wa_ref[...], preferred_element_type=F32)
    b = jnp.dot(x, wb_ref[...], preferred_element_type=F32)
    o_ref[...] = a * jax.nn.sigmoid(b)


def _proj_glu(xn, w, conv_dim, tm, tn):
    n, d = xn.shape
    nb = conv_dim // tn
    return pl.pallas_call(
        _glu_kernel,
        out_shape=jax.ShapeDtypeStruct((n, conv_dim), F32),
        grid=(n // tm, nb),
        in_specs=[pl.BlockSpec((tm, d), lambda i, j: (i, 0)),
                  pl.BlockSpec((d, tn), lambda i, j: (0, j)),
                  pl.BlockSpec((d, tn), lambda i, j: (0, nb + j))],
        out_specs=pl.BlockSpec((tm, tn), lambda i, j: (i, j)),
        compiler_params=_params("parallel", "arbitrary"),
        name="proj_glu",
    )(xn, w, w)


def _headnorm_kernel(x_ref, w_ref, g_ref, *o_refs, scale, hd):
    z = jnp.dot(x_ref[...], w_ref[...], preferred_element_type=F32)
    g = g_ref[...]
    for h in range(z.shape[1] // hd):
        zh = z[:, h * hd:(h + 1) * hd]
        y = zh * lax.rsqrt(jnp.mean(zh * zh, axis=-1, keepdims=True) + EPS) * g
        if scale is not None:
            y = y * scale
        for o_ref in o_refs:
            o_ref[:, h * hd:(h + 1) * hd] = y.astype(o_ref.dtype)


def _proj_headnorm(xn, w, g, col0, width, hd, scale, out_dtypes, tm, tn):
    n, d = xn.shape
    nb = width // tn
    b0 = col0 // tn
    outs = pl.pallas_call(
        functools.partial(_headnorm_kernel, scale=scale, hd=hd),
        out_shape=[jax.ShapeDtypeStruct((n, width), dt) for dt in out_dtypes],
        grid=(n // tm, nb),
        in_specs=[pl.BlockSpec((tm, d), lambda i, j: (i, 0)),
                  pl.BlockSpec((d, tn), lambda i, j: (0, b0 + j)),
                  pl.BlockSpec((1, hd), lambda i, j: (0, 0))],
        out_specs=[pl.BlockSpec((tm, tn), lambda i, j: (i, j)) for _ in out_dtypes],
        compiler_params=_params("parallel", "arbitrary"),
        name="proj_headnorm",
    )(xn, w, g.reshape(1, hd))
    return outs


def _plain_kernel(x_ref, w_ref, *o_refs):
    z = jnp.dot(x_ref[...], w_ref[...], preferred_element_type=F32)
    for o_ref in o_refs:
        o_ref[...] = z.astype(o_ref.dtype)


def _proj_plain(xn, w, col0, width, out_dtypes, tm, tn):
    n, d = xn.shape
    nb = width // tn
    b0 = col0 // tn
    return pl.pallas_call(
        _plain_kernel,
        out_shape=[jax.ShapeDtypeStruct((n, width), dt) for dt in out_dtypes],
        grid=(n // tm, nb),
        in_specs=[pl.BlockSpec((tm, d), lambda i, j: (i, 0)),
                  pl.BlockSpec((d, tn), lambda i, j: (0, b0 + j))],
        out_specs=[pl.BlockSpec((tm, tn), lambda i, j: (i, j)) for _ in out_dtypes],
        compiler_params=_params("parallel", "arbitrary"),
        name="proj_plain",
    )(xn, w)


def _logf_kernel(x_ref, w_ref, b_ref, o_ref):
    z = jnp.dot(x_ref[...], w_ref[...], preferred_element_type=F32) + b_ref[...]
    o_ref[...] = jnp.minimum(z, 0.0) - jnp.log1p(jnp.exp(-jnp.abs(z)))


def _proj_logf(xn, wf, bf, tm):
    n, d = xn.shape
    return pl.pallas_call(
        _logf_kernel,
        out_shape=jax.ShapeDtypeStruct((n, LANES), F32),
        grid=(n // tm,),
        in_specs=[pl.BlockSpec((tm, d), lambda i: (i, 0)),
                  pl.BlockSpec((d, LANES), lambda i: (0, 0)),
                  pl.BlockSpec((1, LANES), lambda i: (0, 0))],
        out_specs=pl.BlockSpec((tm, LANES), lambda i: (i, 0)),
        compiler_params=_params("parallel"),
        name="proj_logf",
    )(xn, wf, bf)


def _conv_kernel(glu_ref, hist_ref, cw_ref, cb_ref, lg_ref, lb_ref, o_ref, full_sc, *, t, tt, width):
    hp = hist_ref.shape[1]
    lanes = glu_ref.shape[2]
    full_sc[0:hp, :] = hist_ref[0]
    full_sc[hp:hp + t, :] = glu_ref[0]
    full_sc[hp + t:hp + t + SUBLANES, :] = jnp.zeros((SUBLANES, lanes), F32)

    def chunk(ci, carry):
        t0 = pl.multiple_of(ci * tt, 16)
        for lt in range(lanes // LANES):
            ls = slice(lt * LANES, (lt + 1) * LANES)
            win = full_sc[pl.ds(t0, tt + hp + SUBLANES), ls]
            y = None
            for r in range(SUBLANES):
                part = None
                for a in range(_cdiv(width + 2, SUBLANES)):
                    j = SUBLANES * a + r
                    if not 2 <= j < width + 2:
                        continue
                    term = win[SUBLANES * a:SUBLANES * a + tt + SUBLANES, :] * cw_ref[j - 2:j - 1, ls]
                    part = term if part is None else part + term
                if part is None:
                    continue
                shifted = part[r:r + tt, :]
                y = shifted if y is None else y + shifted
            y = y + cb_ref[:, ls]
            mu = jnp.mean(y, axis=-1, keepdims=True)
            dlt = y - mu
            var = jnp.mean(dlt * dlt, axis=-1, keepdims=True)
            yn = dlt * lax.rsqrt(var + EPS) * lg_ref[:, ls] + lb_ref[:, ls]
            o_ref[0, pl.ds(t0, tt), ls] = (yn * jax.nn.sigmoid(yn)).astype(o_ref.dtype)
        return carry

    lax.fori_loop(0, t // tt, chunk, 0)


def _conv_module(glu, hist, conv_w, conv_b, ln_g, ln_b):
    b, t, c = glu.shape
    width = conv_w.shape[0]
    hp = hist.shape[1]
    assert hp == 32 and width + 1 == hp and c // CONV_GROUPS == LANES
    tt = _div_tile(t, 64, 16)
    tc = 8 * LANES
    cw = jnp.concatenate([conv_w, jnp.zeros((hp - width, c), F32)], axis=0)
    vec = lambda v: v.reshape(1, c)
    return pl.pallas_call(
        functools.partial(_conv_kernel, t=t, tt=tt, width=width),
        out_shape=jax.ShapeDtypeStruct((b, t, c), BF16),
        grid=(b, c // tc),
        in_specs=[pl.BlockSpec((1, t, tc), lambda i, j: (i, 0, j)),
                  pl.BlockSpec((1, hp, tc), lambda i, j: (i, 0, j)),
                  pl.BlockSpec((hp, tc), lambda i, j: (0, j)),
                  pl.BlockSpec((1, tc), lambda i, j: (0, j)),
                  pl.BlockSpec((1, tc), lambda i, j: (0, j)),
                  pl.BlockSpec((1, tc), lambda i, j: (0, j))],
        out_specs=pl.BlockSpec((1, t, tc), lambda i, j: (i, 0, j)),
        scratch_shapes=[pltpu.VMEM((hp + t + SUBLANES, tc), F32)],
        compiler_params=_params("parallel", "parallel"),
        name="conv_ln_swish",
    )(glu, hist, cw, vec(conv_b), vec(ln_g), vec(ln_b))


def _cumsum_kernel(x_ref, o_ref, *, nblk, blk):
    r = lax.broadcasted_iota(jnp.int32, (blk, blk), 0)
    c = lax.broadcasted_iota(jnp.int32, (blk, blk), 1)
    tri = (c <= r).astype(F32)
    carry = jnp.zeros((1, x_ref.shape[2]), F32)
    for i in range(nblk):
        x = x_ref[0, i * blk:(i + 1) * blk, :]
        o_ref[0, i * blk:(i + 1) * blk, :] = jnp.dot(
            tri, x, precision=lax.Precision.HIGHEST, preferred_element_type=F32) + carry
        carry = carry + jnp.sum(x, axis=0, keepdims=True)


def _cumsum_time(x, blk):
    b, t, h = x.shape
    return pl.pallas_call(
        functools.partial(_cumsum_kernel, nblk=t // blk, blk=blk),
        out_shape=jax.ShapeDtypeStruct((b, t, h), F32),
        grid=(b,),
        in_specs=[pl.BlockSpec((1, t, h), lambda i: (i, 0, 0))],
        out_specs=pl.BlockSpec((1, t, h), lambda i: (i, 0, 0)),
        compiler_params=_params("parallel"),
        name="logf_cumsum",
    )(x)


def _attn_update(q, k, v, cq, ck, mask, m, l, acc):
    s = lax.dot_general(q, k, (((1,), (1,)), ((), ())), preferred_element_type=F32)
    s = s - ck
    if mask is not None:
        s = jnp.where(mask, s, NEG_INF)
    m_new = jnp.maximum(m, jnp.max(s, axis=-1, keepdims=True) + cq)
    alpha = jnp.exp(m - m_new)
    p = jnp.exp(s + (cq - m_new))
    l_new = alpha * l + jnp.sum(p, axis=-1, keepdims=True)
    acc_new = alpha * acc + jnp.dot(p.astype(BF16), v, preferred_element_type=F32)
    return m_new, l_new, acc_new


def _softmax_init(tq, d):
    return (jnp.full((tq, 1), NEG_INF, F32), jnp.zeros((tq, 1), F32), jnp.zeros((tq, d), F32))


def _head_column(c, h):
    sel = lax.broadcasted_iota(jnp.int32, c.shape, 1) == h
    return jnp.sum(jnp.where(sel, c, 0.0), axis=-1, keepdims=True)


def _causal_mask(n):
    r = lax.broadcasted_iota(jnp.int32, (n, n), 0)
    c = lax.broadcasted_iota(jnp.int32, (n, n), 1)
    return c <= r


def _attn_prompt_kernel(q_ref, k_ref, v_ref, cum_ref, o_ref, qa_sc, ka_sc, s_sc, d_sc, p_sc, pd_sc, l_sc,
                        *, t, bq, rs):
    h = pl.program_id(1)
    d = q_ref.shape[2]
    nfull = t // bq
    tail = t - nfull * bq
    nt = (((1,), (1,)), ((), ()))

    lane = lax.broadcasted_iota(jnp.int32, (t, d), 1)
    rem = _head_column(cum_ref[0], h) * LOG2E
    aug = jnp.zeros((t, d), F32)
    for j in range(BIAS_PARTS):
        part = rem.astype(BF16).astype(F32)
        aug = jnp.where(lane == j, part, aug)
        rem = rem - part
    ka_sc[:, 0:d] = k_ref[0]
    ka_sc[:, d:2 * d] = aug.astype(BF16)
    qa_sc[:, 0:d] = q_ref[0]
    qa_sc[:, d:2 * d] = jnp.where(lane < BIAS_PARTS, -1.0, 0.0).astype(BF16)

    blocks = [(i * bq, bq) for i in range(nfull)] + ([(nfull * bq, tail)] if tail else [])
    for q0, nq in blocks:
        q = qa_sc[q0:q0 + nq, :]
        if q0:
            s_sc[0:nq, 0:q0] = lax.dot_general(q, ka_sc[0:q0, :], nt, preferred_element_type=F32)
        d_sc[0:nq, 0:nq] = lax.dot_general(q, ka_sc[q0:q0 + nq, :], nt, preferred_element_type=F32)
        rows = min(rs, nq)

        def softmax_rows(ri, carry, q0=q0, nq=nq, rows=rows):
            r0 = ri * rows
            cq = _head_column(cum_ref[0, pl.ds(q0 + r0, rows), :], h) * LOG2E
            rr = lax.broadcasted_iota(jnp.int32, (rows, nq), 0) + r0
            cc = lax.broadcasted_iota(jnp.int32, (rows, nq), 1)
            sd = jnp.where(cc <= rr, d_sc[pl.ds(r0, rows), 0:nq], NEG_INF)
            mx = jnp.max(sd, axis=-1, keepdims=True)
            if q0:
                sp = s_sc[pl.ds(r0, rows), 0:q0]
                mx = jnp.maximum(mx, jnp.max(sp, axis=-1, keepdims=True))
            shift = cq - (mx + cq)
            pd = jnp.exp2(sd + shift)
            l = jnp.sum(pd, axis=-1, keepdims=True)
            pd_sc[pl.ds(r0, rows), 0:nq] = pd.astype(BF16)
            if q0:
                pp = jnp.exp2(sp + shift)
                l = l + jnp.sum(pp, axis=-1, keepdims=True)
                p_sc[pl.ds(r0, rows), 0:q0] = pp.astype(BF16)
            l_sc[pl.ds(r0, rows), :] = l
            return carry

        for ri in range(nq // rows):
            softmax_rows(ri, 0)
        o = jnp.dot(pd_sc[0:nq, 0:nq], v_ref[0, q0:q0 + nq, :], preferred_element_type=F32)
        if q0:
            o = o + jnp.dot(p_sc[0:nq, 0:q0], v_ref[0, 0:q0, :], preferred_element_type=F32)
        o_ref[0, q0:q0 + nq, :] = (o / l_sc[0:nq, :]).astype(o_ref.dtype)


def _blocked_rows(cum, blk):
    b, t, h = cum.shape
    nblk = _cdiv(t, blk)
    c = jnp.pad(cum, ((0, 0), (0, nblk * blk - t), (0, 0)))
    return c.transpose(0, 2, 1).reshape(b, h, nblk, 1, blk)


def _attn_prompt(q, k, v, cum, n_heads):
    b, t, w = q.shape
    d = w // n_heads
    bq = ATTN_BLOCK
    nfull = t // bq
    assert nfull >= 1 and (t % bq) % 16 == 0
    blk = lambda: pl.BlockSpec((1, t, d), lambda i, j: (i, 0, j))
    return pl.pallas_call(
        functools.partial(_attn_prompt_kernel, t=t, bq=bq, rs=128),
        out_shape=jax.ShapeDtypeStruct((b, t, w), BF16),
        grid=(b, n_heads),
        in_specs=[blk(), blk(), blk(),
                  pl.BlockSpec((1, t, n_heads), lambda i, j: (i, 0, 0))],
        out_specs=blk(),
        scratch_shapes=[pltpu.VMEM((t, 2 * d), BF16), pltpu.VMEM((t, 2 * d), BF16),
                        pltpu.VMEM((bq, nfull * bq), F32), pltpu.VMEM((bq, bq), F32),
                        pltpu.VMEM((bq, nfull * bq), BF16), pltpu.VMEM((bq, bq), BF16),
                        pltpu.VMEM((bq, 1), F32)],
        compiler_params=_params("parallel", "parallel"),
        name="fox_attn_prompt",
    )(q, k, v, cum)


def _attn_sample_kernel(q_ref, ck_ref, cv_ref, k_ref, v_ref, cum_ref, cumt_ref, o_ref, *, past, bk):
    h = pl.program_id(1)
    tq, d = q_ref.shape[1], q_ref.shape[2]
    q = q_ref[0]
    cq = _head_column(cum_ref[0], h)
    carry = _softmax_init(tq, d)
    for kj in range(past // bk):
        kb = ck_ref[0, kj * bk:(kj + 1) * bk, :].astype(BF16)
        vb = cv_ref[0, kj * bk:(kj + 1) * bk, :].astype(BF16)
        carry = _attn_update(q, kb, vb, cq, cumt_ref[0, 0, kj], None, *carry)
    carry = _attn_update(q, k_ref[0], v_ref[0], cq, cumt_ref[0, 0, past // bk][:, :tq], _causal_mask(tq), *carry)
    _, l, acc = carry
    o_ref[0] = (acc / l).astype(o_ref.dtype)


def _attn_sample(q, cache_k, cache_v, k, v, cum, n_heads):
    b, tq, w = q.shape
    d = w // n_heads
    past = cache_k.shape[1]
    bk = _div_tile(past, ATTN_BLOCK, LANES)
    assert tq <= bk
    cumt = _blocked_rows(cum, bk)
    nblk = cumt.shape[2]
    cum_q = cum[:, past:]
    new = lambda: pl.BlockSpec((1, tq, d), lambda i, j: (i, 0, j))
    old = lambda: pl.BlockSpec((1, past, d), lambda i, j: (i, 0, j))
    return pl.pallas_call(
        functools.partial(_attn_sample_kernel, past=past, bk=bk),
        out_shape=jax.ShapeDtypeStruct((b, tq, w), BF16),
        grid=(b, n_heads),
        in_specs=[new(), old(), old(), new(), new(),
                  pl.BlockSpec((1, tq, n_heads), lambda i, j: (i, 0, 0)),
                  pl.BlockSpec((1, 1, nblk, 1, bk), lambda i, j: (i, j, 0, 0, 0))],
        out_specs=new(),
        compiler_params=_params("parallel", "parallel"),
        name="fox_attn_sample",
    )(q, cache_k, cache_v, k, v, cum_q, cumt)


def _outproj_kernel(ap_ref, cp_ref, hp_ref, as_ref, cs_ref, hs_ref, wa_ref, wc_ref, o_ref, *, n_p):
    i = pl.program_id(0)

    def project(a_ref, c_ref, h_ref):
        return (h_ref[...]
                + jnp.dot(a_ref[...], wa_ref[...], preferred_element_type=F32)
                + jnp.dot(c_ref[...], wc_ref[...], preferred_element_type=F32))

    @pl.when(i < n_p)
    def _():
        o_ref[...] = project(ap_ref, cp_ref, hp_ref)

    @pl.when(i == n_p)
    def _():
        o_ref[0:hs_ref.shape[0], :] = project(as_ref, cs_ref, hs_ref)


def _outproj(attn_p, conv_p, h_p, attn_s, conv_s, h_s, w_out, tm, tn):
    npr, wa = attn_p.shape
    ns = attn_s.shape[0]
    wc = conv_p.shape[1]
    d = w_out.shape[1]
    assert wa == wc and npr % tm == 0 and ns <= tm
    n_p = npr // tm
    last = lambda i: jnp.minimum(i, n_p - 1)
    return pl.pallas_call(
        functools.partial(_outproj_kernel, n_p=n_p),
        out_shape=jax.ShapeDtypeStruct(((n_p + 1) * tm, d), F32),
        grid=(n_p + 1, d // tn),
        in_specs=[pl.BlockSpec((tm, wa), lambda i, j: (last(i), 0)),
                  pl.BlockSpec((tm, wc), lambda i, j: (last(i), 0)),
                  pl.BlockSpec((tm, tn), lambda i, j: (last(i), j)),
                  pl.BlockSpec((ns, wa), lambda i, j: (0, 0)),
                  pl.BlockSpec((ns, wc), lambda i, j: (0, 0)),
                  pl.BlockSpec((ns, tn), lambda i, j: (0, j)),
                  pl.BlockSpec((wa, tn), lambda i, j: (0, j)),
                  pl.BlockSpec((wc, tn), lambda i, j: (1, j))],
        out_specs=pl.BlockSpec((tm, tn), lambda i, j: (i, j)),
        compiler_params=_params("parallel", "arbitrary"),
        name="outproj_residual",
    )(attn_p, conv_p, h_p, attn_s, conv_s, h_s, w_out, w_out)


def _router_kernel(h_ref, g_ref, w_ref, b_ref, eid_ref, gate_ref, xp_ref, *, n_groups, per_group):
    x = h_ref[...]
    xn = (x * lax.rsqrt(jnp.mean(x * x, axis=-1, keepdims=True) + EPS) * g_ref[...]).astype(BF16)
    half = x.shape[1] // 2
    bits = pltpu.bitcast(xn.astype(F32), U32)
    xp_ref[...] = bits[:, :half] | (bits[:, half:] >> 16)

    lg = jnp.dot(xn, w_ref[...], preferred_element_type=F32) + b_ref[...]
    lane = lax.broadcasted_iota(jnp.int32, lg.shape, 1)
    ninf = -jnp.inf

    def top1(vals):
        mx = jnp.max(vals, axis=-1, keepdims=True)
        idx = jnp.min(jnp.where(vals == mx, lane, LANES), axis=-1, keepdims=True)
        return mx, idx

    gl = jnp.where(lane < n_groups, lg, ninf)
    gmax, gidx = top1(gl)
    p_g = 1.0 / jnp.sum(jnp.exp(gl - gmax), axis=-1, keepdims=True)
    lo = n_groups + per_group * gidx
    el = jnp.where((lane >= lo) & (lane < lo + per_group), lg, ninf)
    m1, i1 = top1(el)
    m2, i2 = top1(jnp.where(lane == i1, ninf, el))
    e21 = jnp.exp(m2 - m1)
    g1 = p_g / (1.0 + e21)
    g2 = g1 * e21
    eid_ref[...] = jnp.where(lane == 0, i1 - n_groups, jnp.where(lane == 1, i2 - n_groups, 0))
    gate_ref[...] = jnp.where(lane == 0, g1, jnp.where(lane == 1, g2, 0.0))


def _router(h, n, g, w_r, b_r, n_groups, per_group, tm):
    d = h.shape[1]
    row = lambda w: pl.BlockSpec((tm, w), lambda i: (i, 0))
    return pl.pallas_call(
        functools.partial(_router_kernel, n_groups=n_groups, per_group=per_group),
        out_shape=[jax.ShapeDtypeStruct((n, LANES), jnp.int32), jax.ShapeDtypeStruct((n, LANES), F32),
                   jax.ShapeDtypeStruct((n, d // 2), U32)],
        grid=(n // tm,),
        in_specs=[row(d),
                  pl.BlockSpec((1, d), lambda i: (0, 0)),
                  pl.BlockSpec((d, LANES), lambda i: (0, 0)),
                  pl.BlockSpec((1, LANES), lambda i: (0, 0))],
        out_specs=[row(LANES), row(LANES), row(d // 2)],
        compiler_params=_params("parallel"),
        name="router",
    )(h, g.reshape(1, d), w_r, b_r)


def _dispatch_tables(e_id, n_exp):
    n, k = e_id.shape
    m = n * k
    sb = MOE_SUB
    rc = sb * MOE_SUBS_PER_CHUNK
    n_chunks = _cdiv(m, rc) + n_exp
    n_sub_max = _cdiv(m, sb) + n_exp
    e_flat = e_id.reshape(m)
    onehot = (e_flat[:, None] == jnp.arange(n_exp, dtype=jnp.int32)[None, :]).astype(jnp.int32)
    csum = jnp.cumsum(onehot, axis=0)
    counts = csum[-1]
    rank = jnp.take_along_axis(csum, e_flat[:, None], axis=1)[:, 0] - 1
    cpe = (counts + rc - 1) // rc
    chunk_end = jnp.cumsum(cpe)
    chunk_start = chunk_end - cpe
    n_used = chunk_end[-1]
    dest = (chunk_start[e_flat] * rc + rank).astype(jnp.int32)
    c_ar = jnp.arange(n_chunks, dtype=jnp.int32)
    used = c_ar < n_used
    c_cl = jnp.minimum(c_ar, n_used - 1)
    chunk_e = jnp.minimum(jnp.searchsorted(chunk_end, c_cl, side='right'), n_exp - 1).astype(jnp.int32)
    rows = jnp.clip(counts[chunk_e] - (c_cl - chunk_start[chunk_e]) * rc, 0, rc)
    nsub = jnp.where(used, (rows + sb - 1) // sb, 0).astype(jnp.int32)
    in_blk = c_cl.astype(jnp.int32)
    out_blk = jnp.where(used, c_ar, n_chunks).astype(jnp.int32)
    tok = (jnp.arange(m, dtype=jnp.int32) // k)
    src = jnp.zeros((n_chunks * rc,), jnp.int32).at[dest].set(tok)
    sub_active = (jnp.arange(MOE_SUBS_PER_CHUNK, dtype=jnp.int32)[None, :] < nsub[:, None]).reshape(-1)
    n_act = jnp.sum(sub_active.astype(jnp.int32))
    sub_ids = jnp.nonzero(sub_active, size=n_sub_max, fill_value=0)[0].astype(jnp.int32)
    sub_ids = jnp.where(jnp.arange(n_sub_max) < n_act, sub_ids, sub_ids[n_act - 1])
    src_sub = src.reshape(n_chunks * MOE_SUBS_PER_CHUNK, sb)[sub_ids].reshape(-1)
    rows_sub = jnp.clip(rows[:, None] - jnp.arange(MOE_SUBS_PER_CHUNK, dtype=jnp.int32)[None, :] * sb, 0, sb)
    grp_sub = ((rows_sub.reshape(-1)[sub_ids] + GATHER_GROUP - 1) // GATHER_GROUP).astype(jnp.int32)
    return dict(dest=dest, chunk_e=chunk_e, nsub=nsub, in_blk=in_blk, out_blk=out_blk,
                sub_ids=sub_ids, src_sub=src_sub, grp_sub=grp_sub, n_act=n_act.reshape(1).astype(jnp.int32),
                n_chunks=n_chunks, n_sub_max=n_sub_max)


def _gather_kernel(src_ref, sub_ref, nact_ref, grp_ref, xp_hbm, o_ref, buf, sem, *, sb):
    i = pl.program_id(0)
    n_act = nact_ref[0]
    half = xp_hbm.shape[1]

    def issue(step, slot):
        def body(it, carry):
            for u in range(DMA_UNROLL):
                r = it * DMA_UNROLL + u
                tok = src_ref[step * sb + r]
                pltpu.make_async_copy(xp_hbm.at[pl.ds(tok, 1)], buf.at[slot, pl.ds(r, 1)], sem.at[slot]).start()
            return carry
        lax.fori_loop(0, grp_ref[step] * (GATHER_GROUP // DMA_UNROLL), body, 0)

    @pl.when(i == 0)
    def _():
        buf[...] = jnp.zeros(buf.shape, buf.dtype)
        issue(0, 0)

    @pl.when(i + 1 < n_act)
    def _():
        issue(i + 1, (i + 1) % 2)

    @pl.when(i < n_act)
    def _():
        slot = i % 2

        def wait_group(gi, carry):
            pltpu.make_async_copy(xp_hbm.at[pl.ds(0, GATHER_GROUP)], buf.at[slot, pl.ds(0, GATHER_GROUP)],
                                  sem.at[slot]).wait()
            return carry
        lax.fori_loop(0, grp_ref[i], wait_group, 0)
        rows = 32
        for c in range(sb // rows):
            u = buf[slot, c * rows:(c + 1) * rows, :]
            hi = pltpu.bitcast(u & jnp.uint32(0xFFFF0000), F32)
            lo = pltpu.bitcast(u << 16, F32)
            o_ref[c * rows:(c + 1) * rows, 0:half] = hi.astype(o_ref.dtype)
            o_ref[c * rows:(c + 1) * rows, half:2 * half] = lo.astype(o_ref.dtype)


def _gather_rows(xp, tabs):
    n, half = xp.shape
    sb = MOE_SUB
    rows = tabs['n_chunks'] * sb * MOE_SUBS_PER_CHUNK
    return pl.pallas_call(
        functools.partial(_gather_kernel, sb=sb),
        out_shape=jax.ShapeDtypeStruct((rows, 2 * half), BF16),
        grid_spec=pltpu.PrefetchScalarGridSpec(
            num_scalar_prefetch=4,
            grid=(tabs['n_sub_max'],),
            in_specs=[pl.BlockSpec(memory_space=pl.ANY)],
            out_specs=pl.BlockSpec((sb, 2 * half), lambda i, src, sub, nact, grp: (sub[i], 0)),
            scratch_shapes=[pltpu.VMEM((2, sb, half), U32), pltpu.SemaphoreType.DMA((2,))]),
        compiler_params=_params("arbitrary"),
        name="moe_gather",
    )(tabs['src_sub'], tabs['sub_ids'], tabs['n_act'], tabs['grp_sub'], xp)


def _moe_kernel(ce_ref, nsub_ref, ib_ref, ob_ref, x_ref, wg_ref, wu_ref, wd_ref, o_ref, hid_sc, *, sb, n_f, tf):
    c = pl.program_id(0)
    s = pl.program_id(1)
    nsub = nsub_ref[c]

    for r in range(1, MOE_SUBS_PER_CHUNK + 1):
        rows = r * sb

        @pl.when((nsub == r) & (s < n_f))
        def _():
            x = x_ref[0:rows, :]
            g = jnp.dot(x, wg_ref[0].astype(BF16), preferred_element_type=F32)
            u = jnp.dot(x, wu_ref[0].astype(BF16), preferred_element_type=F32)
            hid = (g * jax.nn.sigmoid(g) * u).astype(BF16)
            for f in range(n_f):
                @pl.when(s == f)
                def _():
                    hid_sc[0:rows, f * tf:(f + 1) * tf] = hid

        @pl.when((nsub == r) & (s >= n_f))
        def _():
            o_ref[0:rows, :] = jnp.dot(hid_sc[0:rows, :], wd_ref[0].astype(BF16), preferred_element_type=F32)


def _moe_mlp(x_pad, w_gate, w_up, w_down, tabs):
    n_exp, d, ff = w_gate.shape
    sb = MOE_SUB
    rc = sb * MOE_SUBS_PER_CHUNK
    tf = 256
    td = 1024
    n_f = ff // tf
    n_d = d // td
    n_chunks = tabs['n_chunks']
    f_idx = lambda c, s, ns: jnp.where(ns[c] > 0, jnp.minimum(s, n_f - 1), n_f - 1)
    d_idx = lambda c, s, ns: jnp.where(ns[c] > 0, jnp.maximum(s - n_f, 0), n_d - 1)
    wmap = lambda c, s, ce, ns, ib, ob: (ce[c], 0, f_idx(c, s, ns))
    return pl.pallas_call(
        functools.partial(_moe_kernel, sb=sb, n_f=n_f, tf=tf),
        out_shape=jax.ShapeDtypeStruct(((n_chunks + 1) * rc, d), F32),
        grid_spec=pltpu.PrefetchScalarGridSpec(
            num_scalar_prefetch=4,
            grid=(n_chunks, n_f + n_d),
            in_specs=[pl.BlockSpec((rc, d), lambda c, s, ce, ns, ib, ob: (ib[c], 0)),
                      pl.BlockSpec((1, d, tf), wmap),
                      pl.BlockSpec((1, d, tf), wmap),
                      pl.BlockSpec((1, ff, td), lambda c, s, ce, ns, ib, ob: (ce[c], 0, d_idx(c, s, ns)))],
            out_specs=pl.BlockSpec((rc, td), lambda c, s, ce, ns, ib, ob:
                                   (ob[c], jnp.where(ns[c] > 0, jnp.maximum(s - n_f, 0), 0))),
            scratch_shapes=[pltpu.VMEM((rc, ff), BF16)]),
        compiler_params=_params("arbitrary", "arbitrary"),
        name="moe_mlp",
    )(tabs['chunk_e'], tabs['nsub'], tabs['in_blk'], tabs['out_blk'], x_pad, w_gate, w_up, w_down)


def _combine_kernel(dest_ref, row0_ref, y_hbm, h_hbm, gate_hbm, o_ref, ybuf, hbuf, gbuf, sem, *, tm):
    i = pl.program_id(0)
    n = pl.num_programs(0)

    def contiguous(step, slot):
        r0 = pl.multiple_of(row0_ref[step], SUBLANES)
        return (pltpu.make_async_copy(h_hbm.at[pl.ds(r0, tm)], hbuf.at[slot], sem.at[slot]),
                pltpu.make_async_copy(gate_hbm.at[pl.ds(r0, tm)], gbuf.at[slot], sem.at[slot]))

    def issue(step, slot):
        for cp in contiguous(step, slot):
            cp.start()
        base = row0_ref[step] * TOP_K

        def body(it, carry):
            for u in range(DMA_UNROLL // TOP_K):
                r = it * (DMA_UNROLL // TOP_K) + u
                for k in range(TOP_K):
                    row = dest_ref[base + r * TOP_K + k]
                    pltpu.make_async_copy(y_hbm.at[pl.ds(row, 1)], ybuf.at[slot, k, pl.ds(r, 1)],
                                          sem.at[slot]).start()
            return carry
        lax.fori_loop(0, tm * TOP_K // DMA_UNROLL, body, 0)

    @pl.when(i == 0)
    def _():
        issue(0, 0)

    @pl.when(i + 1 < n)
    def _():
        issue(i + 1, (i + 1) % 2)

    slot = i % 2
    for cp in contiguous(i, slot):
        cp.wait()
    for k in range(TOP_K):
        pltpu.make_async_copy(y_hbm.at[pl.ds(0, tm)], ybuf.at[slot, k], sem.at[slot]).wait()
    gate = gbuf[slot]
    out = hbuf[slot]
    for k in range(TOP_K):
        out = out + gate[:, k:k + 1] * ybuf[slot, k]
    o_ref[...] = out


def _combine(y_pad, h, gate, dest, row0, tm):
    d = h.shape[1]
    n_tiles = row0.shape[0]
    return pl.pallas_call(
        functools.partial(_combine_kernel, tm=tm),
        out_shape=jax.ShapeDtypeStruct((n_tiles * tm, d), F32),
        grid_spec=pltpu.PrefetchScalarGridSpec(
            num_scalar_prefetch=2,
            grid=(n_tiles,),
            in_specs=[pl.BlockSpec(memory_space=pl.ANY), pl.BlockSpec(memory_space=pl.ANY),
                      pl.BlockSpec(memory_space=pl.ANY)],
            out_specs=pl.BlockSpec((tm, d), lambda i, dest, row0: (i, 0)),
            scratch_shapes=[pltpu.VMEM((2, TOP_K, tm, d), F32), pltpu.VMEM((2, tm, d), F32),
                            pltpu.VMEM((2, tm, LANES), F32), pltpu.SemaphoreType.DMA((2,))]),
        compiler_params=_params("arbitrary"),
        name="moe_combine",
    )(dest, row0, y_pad, h, gate)


def _mixer_inputs(h, norm_g, w_in_bf, wf, bf, q_g, k_g, conv_dim, attn_w, n_heads, tm, q_scale):
    hd = attn_w // n_heads
    tn = 512
    xn = _rmsnorm(h, norm_g, _div_tile(h.shape[0], 256, 8))
    glu = _proj_glu(xn, w_in_bf, conv_dim, tm, tn // 2)
    c1 = 2 * conv_dim
    (q,) = _proj_headnorm(xn, w_in_bf, q_g, c1, attn_w, hd, q_scale, (BF16,), tm, tn)
    k32, k16 = _proj_headnorm(xn, w_in_bf, k_g, c1 + attn_w, attn_w, hd, None, (F32, BF16), tm, tn)
    v32, v16 = _proj_plain(xn, w_in_bf, c1 + 2 * attn_w, attn_w, (F32, BF16), tm, tn)
    logf = _proj_logf(xn, wf, bf, tm)[:, :n_heads]
    return glu, q, k32, k16, v32, v16, logf


def kernel(x_prompt, x_sample, cache_k, cache_v, cache_logf, cache_conv, meta_tokens, norm_mix_g, w_in, b_forget,
           q_norm_g, k_norm_g, conv_w, conv_b, conv_ln_g, conv_ln_b, w_out, norm_ffn_g, w_router_group,
           b_router_group, w_router_expert, b_router_expert, w_gate, w_up, w_down):
    depth = w_in.shape[0]
    assert depth == 1
    bp, seq, d = x_prompt.shape
    bs, tq, _ = x_sample.shape
    n_heads = cache_k.shape[3]
    hd = cache_k.shape[4]
    attn_w = n_heads * hd
    conv_dim = conv_w.shape[2]
    hist_len = conv_w.shape[1] - 1
    past = cache_k.shape[2]
    n_groups = w_router_group.shape[2]
    n_exp = w_router_expert.shape[2]
    per_group = n_exp // n_groups
    assert n_groups + n_exp <= LANES and 2 * conv_dim % 512 == 0 and attn_w % 512 == 0
    t = N_META + seq
    npr = bp * t
    ns = bs * tq
    n_all = npr + ns

    w_in_bf = w_in[0].astype(BF16)
    c4 = 2 * conv_dim + 3 * attn_w
    wf = jnp.pad(w_in[0][:, c4:], ((0, 0), (0, LANES - n_heads))).astype(BF16)
    bf = jnp.pad(b_forget[0], (0, LANES - n_heads)).reshape(1, LANES)
    w_out_bf = w_out[0].astype(BF16)
    w_r = jnp.pad(jnp.concatenate([w_router_group[0], w_router_expert[0]], axis=1),
                  ((0, 0), (0, LANES - n_groups - n_exp))).astype(BF16)
    b_r = jnp.pad(jnp.concatenate([b_router_group[0], b_router_expert[0]]),
                  (0, LANES - n_groups - n_exp)).reshape(1, LANES)

    h_p = jnp.concatenate([jnp.broadcast_to(meta_tokens[None], (bp, N_META, d)), x_prompt], axis=1).reshape(npr, d)
    h_s = x_sample.reshape(ns, d)

    tm_p = _div_tile(npr, 1400, 16)
    tm_s = _div_tile(ns, 1400, 16)
    mix = functools.partial(_mixer_inputs, norm_g=norm_mix_g[0], w_in_bf=w_in_bf, wf=wf, bf=bf, q_g=q_norm_g[0],
                            k_g=k_norm_g[0], conv_dim=conv_dim, attn_w=attn_w, n_heads=n_heads)
    glu_p, q_p, k32_p, k16_p, v32_p, v16_p, logf_p = mix(h_p, tm=tm_p, q_scale=hd ** -0.5 * LOG2E)
    glu_s, q_s, k32_s, k16_s, v32_s, v16_s, logf_s = mix(h_s, tm=tm_s, q_scale=hd ** -0.5)

    glu_p3 = glu_p.reshape(bp, t, conv_dim)
    glu_s3 = glu_s.reshape(bs, tq, conv_dim)
    hist_p = jnp.zeros((bp, hist_len + 2, conv_dim), F32)
    hist_s = jnp.pad(cache_conv[0], ((0, 0), (2, 0), (0, 0)))
    conv_args = (conv_w[0], conv_b[0], conv_ln_g[0], conv_ln_b[0])
    conv_p = _conv_module(glu_p3, hist_p, *conv_args)
    conv_s = _conv_module(glu_s3, hist_s, *conv_args)

    logf_p3 = logf_p.reshape(bp, t, n_heads)
    logf_s3 = logf_s.reshape(bs, tq, n_heads)
    cblk = 256
    pad_t = lambda a: jnp.pad(a, ((0, 0), (0, _cdiv(a.shape[1], cblk) * cblk - a.shape[1]), (0, 0)))
    cum_p = _cumsum_time(pad_t(logf_p3), cblk)[:, :t]
    lf_all = jnp.concatenate([cache_logf[0], logf_s3], axis=1)
    cum_s = _cumsum_time(pad_t(lf_all), cblk)[:, :past + tq]
    r3 = lambda a, b_, t_: a.reshape(b_, t_, attn_w)
    attn_p = _attn_prompt(r3(q_p, bp, t), r3(k16_p, bp, t), r3(v16_p, bp, t), cum_p, n_heads)
    attn_s = _attn_sample(r3(q_s, bs, tq), cache_k[0].reshape(bs, past, attn_w), cache_v[0].reshape(bs, past, attn_w),
                          r3(k16_s, bs, tq), r3(v16_s, bs, tq), cum_s, n_heads)

    tn_o = 512
    h2 = _outproj(attn_p.reshape(npr, attn_w), conv_p.reshape(npr, conv_dim), h_p,
                  attn_s.reshape(ns, attn_w), conv_s.reshape(ns, conv_dim), h_s, w_out_bf, tm_p, tn_o)

    tm_r = _div_tile(n_all, 128, 8)
    eid, gate, xp = _router(h2, n_all, norm_ffn_g[0], w_r, b_r, n_groups, per_group, tm_r)
    tabs = _dispatch_tables(eid[:, :TOP_K], n_exp)
    x_pad = _gather_rows(xp, tabs)
    y_pad = _moe_mlp(x_pad, w_gate[0], w_up[0], w_down[0], tabs)

    tm_cp = _div_tile(seq, 128, 8)
    tiles_b = seq // tm_cp
    i_p = jnp.arange(bp * tiles_b, dtype=jnp.int32)
    row0_p = (i_p // tiles_b) * t + N_META + (i_p % tiles_b) * tm_cp
    tm_cs = _div_tile(ns, 128, 8)
    row0_s = npr + jnp.arange(ns // tm_cs, dtype=jnp.int32) * tm_cs
    y_p = _combine(y_pad, h2, gate, tabs['dest'], row0_p, tm_cp).reshape(bp, seq, d)
    y_s = _combine(y_pad, h2, gate, tabs['dest'], row0_s, tm_cs).reshape(bs, tq, d)

    st = lambda a, b_, t_: a.reshape(1, b_, t_, n_heads, hd)
    return (y_p, y_s,
            st(k32_p, bp, t), st(v32_p, bp, t), logf_p3[None], glu_p3[:, t - hist_len:][None],
            st(k32_s, bs, tq), st(v32_s, bs, tq), logf_s3[None],
            jnp.concatenate([cache_conv[0], glu_s3], axis=1)[:, -hist_len:][None])
```

```python
import functools

import jax
import jax.numpy as jnp
from jax import lax
from jax.experimental import pallas as pl
from jax.experimental.pallas import tpu as pltpu

F32 = jnp.float32
BF16 = jnp.bfloat16
U32 = jnp.uint32

N_META = 16
CONV_GROUPS = 16
TOP_K = 2
EPS = 1e-6
NEG_INF = -1e30

LANES = 128
SUBLANES = 8
ATTN_BLOCK = 512
MOE_SUB = 128
MOE_SUBS_PER_CHUNK = 6
DMA_UNROLL = 8
GATHER_GROUP = 32
LOG2E = 1.4426950408889634
BIAS_PARTS = 3
VMEM_LIMIT = 56 * 1024 * 1024


def _cdiv(a, b):
    return -(-a // b)


def _gcd(a, b):
    while b:
        a, b = b, a % b
    return a


def _div_tile(n, target, mult):
    best = None
    for d in range(mult, min(n, target) + 1, mult):
        if n % d == 0:
            best = d
    assert best is not None, (n, target, mult)
    return best


def _params(*sem):
    return pltpu.CompilerParams(dimension_semantics=sem, vmem_limit_bytes=VMEM_LIMIT)


def _rmsnorm_kernel(x_ref, g_ref, o_ref):
    x = x_ref[...]
    ms = jnp.mean(x * x, axis=-1, keepdims=True)
    o_ref[...] = (x * lax.rsqrt(ms + EPS) * g_ref[...]).astype(o_ref.dtype)


def _rmsnorm(x, g, tm):
    n, d = x.shape
    return pl.pallas_call(
        _rmsnorm_kernel,
        out_shape=jax.ShapeDtypeStruct((n, d), BF16),
        grid=(n // tm,),
        in_specs=[pl.BlockSpec((tm, d), lambda i: (i, 0)), pl.BlockSpec((1, d), lambda i: (0, 0))],
        out_specs=pl.BlockSpec((tm, d), lambda i: (i, 0)),
        compiler_params=_params("parallel"),
        name="rmsnorm",
    )(x, g.reshape(1, d))


def _glu_kernel(x_ref, wa_ref, wb_ref, o_ref):
    x = x_ref[...]
    a = jnp.dot(x, wa_ref[...], preferred_element_type=F32)
    b = jnp.dot(x, wb_ref[...], preferred_element_type=F32)
    o_ref[...] = a * jax.nn.sigmoid(b)


def _proj_glu(xn, w, conv_dim, tm, tn):
    n, d = xn.shape
    nb = conv_dim // tn
    return pl.pallas_call(
        _glu_kernel,
        out_shape=jax.ShapeDtypeStruct((n, conv_dim), F32),
        grid=(n // tm, nb),
        in_specs=[pl.BlockSpec((tm, d), lambda i, j: (i, 0)),
                  pl.BlockSpec((d, tn), lambda i, j: (0, j)),
                  pl.BlockSpec((d, tn), lambda i, j: (0, nb + j))],
        out_specs=pl.BlockSpec((tm, tn), lambda i, j: (i, j)),
        compiler_params=_params("parallel", "arbitrary"),
        name="proj_glu",
    )(xn, w, w)


def _headnorm_kernel(x_ref, w_ref, g_ref, *o_refs, scale, hd):
    z = jnp.dot(x_ref[...], w_ref[...], preferred_element_type=F32)
    g = g_ref[...]
    for h in range(z.shape[1] // hd):
        zh = z[:, h * hd:(h + 1) * hd]
        y = zh * lax.rsqrt(jnp.mean(zh * zh, axis=-1, keepdims=True) + EPS) * g
        if scale is not None:
            y = y * scale
        for o_ref in o_refs:
            o_ref[:, h * hd:(h + 1) * hd] = y.astype(o_ref.dtype)


def _proj_headnorm(xn, w, g, col0, width, hd, scale, out_dtypes, tm, tn):
    n, d = xn.shape
    nb = width // tn
    b0 = col0 // tn
    outs = pl.pallas_call(
        functools.partial(_headnorm_kernel, scale=scale, hd=hd),
        out_shape=[jax.ShapeDtypeStruct((n, width), dt) for dt in out_dtypes],
        grid=(n // tm, nb),
        in_specs=[pl.BlockSpec((tm, d), lambda i, j: (i, 0)),
                  pl.BlockSpec((d, tn), lambda i, j: (0, b0 + j)),
                  pl.BlockSpec((1, hd), lambda i, j: (0, 0))],
        out_specs=[pl.BlockSpec((tm, tn), lambda i, j: (i, j)) for _ in out_dtypes],
        compiler_params=_params("parallel", "arbitrary"),
        name="proj_headnorm",
    )(xn, w, g.reshape(1, hd))
    return outs


def _plain_kernel(x_ref, w_ref, *o_refs):
    z = jnp.dot(x_ref[...], w_ref[...], preferred_element_type=F32)
    for o_ref in o_refs:
        o_ref[...] = z.astype(o_ref.dtype)


def _proj_plain(xn, w, col0, width, out_dtypes, tm, tn):
    n, d = xn.shape
    nb = width // tn
    b0 = col0 // tn
    return pl.pallas_call(
        _plain_kernel,
        out_shape=[jax.ShapeDtypeStruct((n, width), dt) for dt in out_dtypes],
        grid=(n // tm, nb),
        in_specs=[pl.BlockSpec((tm, d), lambda i, j: (i, 0)),
                  pl.BlockSpec((d, tn), lambda i, j: (0, b0 + j))],
        out_specs=[pl.BlockSpec((tm, tn), lambda i, j: (i, j)) for _ in out_dtypes],
        compiler_params=_params("parallel", "arbitrary"),
        name="proj_plain",
    )(xn, w)


def _logf_kernel(x_ref, w_ref, b_ref, o_ref):
    z = jnp.dot(x_ref[...], w_ref[...], preferred_element_type=F32) + b_ref[...]
    o_ref[...] = jnp.minimum(z, 0.0) - jnp.log1p(jnp.exp(-jnp.abs(z)))


def _proj_logf(xn, wf, bf, tm):
    n, d = xn.shape
    return pl.pallas_call(
        _logf_kernel,
        out_shape=jax.ShapeDtypeStruct((n, LANES), F32),
        grid=(n // tm,),
        in_specs=[pl.BlockSpec((tm, d), lambda i: (i, 0)),
                  pl.BlockSpec((d, LANES), lambda i: (0, 0)),
                  pl.BlockSpec((1, LANES), lambda i: (0, 0))],
        out_specs=pl.BlockSpec((tm, LANES), lambda i: (i, 0)),
        compiler_params=_params("parallel"),
        name="proj_logf",
    )(xn, wf, bf)


def _conv_kernel(glu_ref, hist_ref, cw_ref, cb_ref, lg_ref, lb_ref, o_ref, full_sc, *, t, tt, width):
    hp = hist_ref.shape[1]
    lanes = glu_ref.shape[2]
    full_sc[0:hp, :] = hist_ref[0]
    full_sc[hp:hp + t, :] = glu_ref[0]
    full_sc[hp + t:hp + t + SUBLANES, :] = jnp.zeros((SUBLANES, lanes), F32)

    def chunk(ci, carry):
        t0 = pl.multiple_of(ci * tt, 16)
        for lt in range(lanes // LANES):
            ls = slice(lt * LANES, (lt + 1) * LANES)
            win = full_sc[pl.ds(t0, tt + hp + SUBLANES), ls]
            y = None
            for r in range(SUBLANES):
                part = None
                for a in range(_cdiv(width + 2, SUBLANES)):
                    j = SUBLANES * a + r
                    if not 2 <= j < width + 2:
                        continue
                    term = win[SUBLANES * a:SUBLANES * a + tt + SUBLANES, :] * cw_ref[j - 2:j - 1, ls]
                    part = term if part is None else part + term
                if part is None:
                    continue
                shifted = part[r:r + tt, :]
                y = shifted if y is None else y + shifted
            y = y + cb_ref[:, ls]
            mu = jnp.mean(y, axis=-1, keepdims=True)
            dlt = y - mu
            var = jnp.mean(dlt * dlt, axis=-1, keepdims=True)
            yn = dlt * lax.rsqrt(var + EPS) * lg_ref[:, ls] + lb_ref[:, ls]
            o_ref[0, pl.ds(t0, tt), ls] = (yn * jax.nn.sigmoid(yn)).astype(o_ref.dtype)
        return carry

    lax.fori_loop(0, t // tt, chunk, 0)


def _conv_module(glu, hist, conv_w, conv_b, ln_g, ln_b):
    b, t, c = glu.shape
    width = conv_w.shape[0]
    hp = hist.shape[1]
    assert hp == 32 and width + 1 == hp and c // CONV_GROUPS == LANES
    tt = _div_tile(t, 64, 16)
    tc = 8 * LANES
    cw = jnp.concatenate([conv_w, jnp.zeros((hp - width, c), F32)], axis=0)
    vec = lambda v: v.reshape(1, c)
    return pl.pallas_call(
        functools.partial(_conv_kernel, t=t, tt=tt, width=width),
        out_shape=jax.ShapeDtypeStruct((b, t, c), BF16),
        grid=(b, c // tc),
        in_specs=[pl.BlockSpec((1, t, tc), lambda i, j: (i, 0, j)),
                  pl.BlockSpec((1, hp, tc), lambda i, j: (i, 0, j)),
                  pl.BlockSpec((hp, tc), lambda i, j: (0, j)),
                  pl.BlockSpec((1, tc), lambda i, j: (0, j)),
                  pl.BlockSpec((1, tc), lambda i, j: (0, j)),
                  pl.BlockSpec((1, tc), lambda i, j: (0, j))],
        out_specs=pl.BlockSpec((1, t, tc), lambda i, j: (i, 0, j)),
        scratch_shapes=[pltpu.VMEM((hp + t + SUBLANES, tc), F32)],
        compiler_params=_params("parallel", "parallel"),
        name="conv_ln_swish",
    )(glu, hist, cw, vec(conv_b), vec(ln_g), vec(ln_b))


def _cumsum_kernel(x_ref, o_ref, *, nblk, blk):
    r = lax.broadcasted_iota(jnp.int32, (blk, blk), 0)
    c = lax.broadcasted_iota(jnp.int32, (blk, blk), 1)
    tri = (c <= r).astype(F32)
    carry = jnp.zeros((1, x_ref.shape[2]), F32)
    for i in range(nblk):
        x = x_ref[0, i * blk:(i + 1) * blk, :]
        o_ref[0, i * blk:(i + 1) * blk, :] = jnp.dot(
            tri, x, precision=lax.Precision.HIGHEST, preferred_element_type=F32) + carry
        carry = carry + jnp.sum(x, axis=0, keepdims=True)


def _cumsum_time(x, blk):
    b, t, h = x.shape
    return pl.pallas_call(
        functools.partial(_cumsum_kernel, nblk=t // blk, blk=blk),
        out_shape=jax.ShapeDtypeStruct((b, t, h), F32),
        grid=(b,),
        in_specs=[pl.BlockSpec((1, t, h), lambda i: (i, 0, 0))],
        out_specs=pl.BlockSpec((1, t, h), lambda i: (i, 0, 0)),
        compiler_params=_params("parallel"),
        name="logf_cumsum",
    )(x)


def _attn_update(q, k, v, cq, ck, mask, m, l, acc):
    s = lax.dot_general(q, k, (((1,), (1,)), ((), ())), preferred_element_type=F32)
    s = s - ck
    if mask is not None:
        s = jnp.where(mask, s, NEG_INF)
    m_new = jnp.maximum(m, jnp.max(s, axis=-1, keepdims=True) + cq)
    alpha = jnp.exp(m - m_new)
    p = jnp.exp(s + (cq - m_new))
    l_new = alpha * l + jnp.sum(p, axis=-1, keepdims=True)
    acc_new = alpha * acc + jnp.dot(p.astype(BF16), v, preferred_element_type=F32)
    return m_new, l_new, acc_new


def _softmax_init(tq, d):
    return (jnp.full((tq, 1), NEG_INF, F32), jnp.zeros((tq, 1), F32), jnp.zeros((tq, d), F32))


def _head_column(c, h):
    sel = lax.broadcasted_iota(jnp.int32, c.shape, 1) == h
    return jnp.sum(jnp.where(sel, c, 0.0), axis=-1, keepdims=True)


def _causal_mask(n):
    r = lax.broadcasted_iota(jnp.int32, (n, n), 0)
    c = lax.broadcasted_iota(jnp.int32, (n, n), 1)
    return c <= r


def _attn_prompt_kernel(q_ref, k_ref, v_ref, cum_ref, o_ref, qa_sc, ka_sc, s_sc, d_sc, p_sc, pd_sc, l_sc,
                        *, t, bq, rs):
    h = pl.program_id(1)
    d = q_ref.shape[2]
    nfull = t // bq
    tail = t - nfull * bq
    nt = (((1,), (1,)), ((), ()))

    lane = lax.broadcasted_iota(jnp.int32, (t, d), 1)
    rem = _head_column(cum_ref[0], h) * LOG2E
    aug = jnp.zeros((t, d), F32)
    for j in range(BIAS_PARTS):
        part = rem.astype(BF16).astype(F32)
        aug = jnp.where(lane == j, part, aug)
        rem = rem - part
    ka_sc[:, 0:d] = k_ref[0]
    ka_sc[:, d:2 * d] = aug.astype(BF16)
    qa_sc[:, 0:d] = q_ref[0]
    qa_sc[:, d:2 * d] = jnp.where(lane < BIAS_PARTS, -1.0, 0.0).astype(BF16)

    blocks = [(i * bq, bq) for i in range(nfull)] + ([(nfull * bq, tail)] if tail else [])
    for q0, nq in blocks:
        q = qa_sc[q0:q0 + nq, :]
        if q0:
            s_sc[0:nq, 0:q0] = lax.dot_general(q, ka_sc[0:q0, :], nt, preferred_element_type=F32)
        d_sc[0:nq, 0:nq] = lax.dot_general(q, ka_sc[q0:q0 + nq, :], nt, preferred_element_type=F32)
        rows = min(rs, nq)

        def softmax_rows(ri, carry, q0=q0, nq=nq, rows=rows):
            r0 = ri * rows
            cq = _head_column(cum_ref[0, pl.ds(q0 + r0, rows), :], h) * LOG2E
            rr = lax.broadcasted_iota(jnp.int32, (rows, nq), 0) + r0
            cc = lax.broadcasted_iota(jnp.int32, (rows, nq), 1)
            sd = jnp.where(cc <= rr, d_sc[pl.ds(r0, rows), 0:nq], NEG_INF)
            mx = jnp.max(sd, axis=-1, keepdims=True)
            if q0:
                sp = s_sc[pl.ds(r0, rows), 0:q0]
                mx = jnp.maximum(mx, jnp.max(sp, axis=-1, keepdims=True))
            shift = cq - (mx + cq)
            pd = jnp.exp2(sd + shift)
            l = jnp.sum(pd, axis=-1, keepdims=True)
            pd_sc[pl.ds(r0, rows), 0:nq] = pd.astype(BF16)
            if q0:
                pp = jnp.exp2(sp + shift)
                l = l + jnp.sum(pp, axis=-1, keepdims=True)
                p_sc[pl.ds(r0, rows), 0:q0] = pp.astype(BF16)
            l_sc[pl.ds(r0, rows), :] = l
            return carry

        for ri in range(nq // rows):
            softmax_rows(ri, 0)
        o = jnp.dot(pd_sc[0:nq, 0:nq], v_ref[0, q0:q0 + nq, :], preferred_element_type=F32)
        if q0:
            o = o + jnp.dot(p_sc[0:nq, 0:q0], v_ref[0, 0:q0, :], preferred_element_type=F32)
        o_ref[0, q0:q0 + nq, :] = (o / l_sc[0:nq, :]).astype(o_ref.dtype)


def _blocked_rows(cum, blk):
    b, t, h = cum.shape
    nblk = _cdiv(t, blk)
    c = jnp.pad(cum, ((0, 0), (0, nblk * blk - t), (0, 0)))
    return c.transpose(0, 2, 1).reshape(b, h, nblk, 1, blk)


def _attn_prompt(q, k, v, cum, n_heads):
    b, t, w = q.shape
    d = w // n_heads
    bq = ATTN_BLOCK
    nfull = t // bq
    assert nfull >= 1 and (t % bq) % 16 == 0
    blk = lambda: pl.BlockSpec((1, t, d), lambda i, j: (i, 0, j))
    return pl.pallas_call(
        functools.partial(_attn_prompt_kernel, t=t, bq=bq, rs=128),
        out_shape=jax.ShapeDtypeStruct((b, t, w), BF16),
        grid=(b, n_heads),
        in_specs=[blk(), blk(), blk(),
                  pl.BlockSpec((1, t, n_heads), lambda i, j: (i, 0, 0))],
        out_specs=blk(),
        scratch_shapes=[pltpu.VMEM((t, 2 * d), BF16), pltpu.VMEM((t, 2 * d), BF16),
                        pltpu.VMEM((bq, nfull * bq), F32), pltpu.VMEM((bq, bq), F32),
                        pltpu.VMEM((bq, nfull * bq), BF16), pltpu.VMEM((bq, bq), BF16),
                        pltpu.VMEM((bq, 1), F32)],
        compiler_params=_params("parallel", "parallel"),
        name="fox_attn_prompt",
    )(q, k, v, cum)


def _attn_sample_kernel(q_ref, ck_ref, cv_ref, k_ref, v_ref, cum_ref, cumt_ref, o_ref, *, past, bk, hpb, d):
    tq = q_ref.shape[1]
    for hh in range(hpb):
        cols = slice(hh * d, (hh + 1) * d)
        q = q_ref[0, :, cols]
        cq = _head_column(cum_ref[0], pl.program_id(1) * hpb + hh)
        carry = _softmax_init(tq, d)
        for kj in range(past // bk):
            kb = ck_ref[0, kj * bk:(kj + 1) * bk, cols].astype(BF16)
            vb = cv_ref[0, kj * bk:(kj + 1) * bk, cols].astype(BF16)
            carry = _attn_update(q, kb, vb, cq, cumt_ref[0, hh, kj], None, *carry)
        carry = _attn_update(q, k_ref[0, :, cols], v_ref[0, :, cols], cq, cumt_ref[0, hh, past // bk][:, :tq],
                             _causal_mask(tq), *carry)
        _, l, acc = carry
        o_ref[0, :, cols] = (acc / l).astype(o_ref.dtype)


def _attn_sample(q, cache_k, cache_v, k, v, cum, n_heads):
    b, tq, w = q.shape
    d = w // n_heads
    past = cache_k.shape[1]
    bk = _div_tile(past, ATTN_BLOCK, LANES)
    assert tq <= bk
    cumt = _blocked_rows(cum, bk)
    nblk = cumt.shape[2]
    cum_q = cum[:, past:]
    hpb = 4 if n_heads % 4 == 0 else 1
    new = lambda: pl.BlockSpec((1, tq, hpb * d), lambda i, j: (i, 0, j))
    old = lambda: pl.BlockSpec((1, past, hpb * d), lambda i, j: (i, 0, j))
    return pl.pallas_call(
        functools.partial(_attn_sample_kernel, past=past, bk=bk, hpb=hpb, d=d),
        out_shape=jax.ShapeDtypeStruct((b, tq, w), BF16),
        grid=(b, n_heads // hpb),
        in_specs=[new(), old(), old(), new(), new(),
                  pl.BlockSpec((1, tq, n_heads), lambda i, j: (i, 0, 0)),
                  pl.BlockSpec((1, hpb, nblk, 1, bk), lambda i, j: (i, j, 0, 0, 0))],
        out_specs=new(),
        compiler_params=_params("parallel", "parallel"),
        name="fox_attn_sample",
    )(q, cache_k, cache_v, k, v, cum_q, cumt)


def _outproj_kernel(ap_ref, cp_ref, hp_ref, as_ref, cs_ref, hs_ref, wa_ref, wc_ref, o_ref, *, n_p):
    i = pl.program_id(0)

    def project(a_ref, c_ref, h_ref):
        return (h_ref[...]
                + jnp.dot(a_ref[...], wa_ref[...], preferred_element_type=F32)
                + jnp.dot(c_ref[...], wc_ref[...], preferred_element_type=F32))

    @pl.when(i < n_p)
    def _():
        o_ref[...] = project(ap_ref, cp_ref, hp_ref)

    @pl.when(i == n_p)
    def _():
        o_ref[0:hs_ref.shape[0], :] = project(as_ref, cs_ref, hs_ref)


def _outproj(attn_p, conv_p, h_p, attn_s, conv_s, h_s, w_out, tm, tn):
    npr, wa = attn_p.shape
    ns = attn_s.shape[0]
    wc = conv_p.shape[1]
    d = w_out.shape[1]
    assert wa == wc and npr % tm == 0 and ns <= tm
    n_p = npr // tm
    last = lambda i: jnp.minimum(i, n_p - 1)
    return pl.pallas_call(
        functools.partial(_outproj_kernel, n_p=n_p),
        out_shape=jax.ShapeDtypeStruct(((n_p + 1) * tm, d), F32),
        grid=(n_p + 1, d // tn),
        in_specs=[pl.BlockSpec((tm, wa), lambda i, j: (last(i), 0)),
                  pl.BlockSpec((tm, wc), lambda i, j: (last(i), 0)),
                  pl.BlockSpec((tm, tn), lambda i, j: (last(i), j)),
                  pl.BlockSpec((ns, wa), lambda i, j: (0, 0)),
                  pl.BlockSpec((ns, wc), lambda i, j: (0, 0)),
                  pl.BlockSpec((ns, tn), lambda i, j: (0, j)),
                  pl.BlockSpec((wa, tn), lambda i, j: (0, j)),
                  pl.BlockSpec((wc, tn), lambda i, j: (1, j))],
        out_specs=pl.BlockSpec((tm, tn), lambda i, j: (i, j)),
        compiler_params=_params("parallel", "arbitrary"),
        name="outproj_residual",
    )(attn_p, conv_p, h_p, attn_s, conv_s, h_s, w_out, w_out)


def _router_kernel(h_ref, g_ref, w_ref, b_ref, eid_ref, gate_ref, xp_ref, *, n_groups, per_group):
    x = h_ref[...]
    xn = (x * lax.rsqrt(jnp.mean(x * x, axis=-1, keepdims=True) + EPS) * g_ref[...]).astype(BF16)
    half = x.shape[1] // 2
    bits = pltpu.bitcast(xn.astype(F32), U32)
    xp_ref[...] = bits[:, :half] | (bits[:, half:] >> 16)

    lg = jnp.dot(xn, w_ref[...], preferred_element_type=F32) + b_ref[...]
    lane = lax.broadcasted_iota(jnp.int32, lg.shape, 1)
    ninf = -jnp.inf

    def top1(vals):
        mx = jnp.max(vals, axis=-1, keepdims=True)
        idx = jnp.min(jnp.where(vals == mx, lane, LANES), axis=-1, keepdims=True)
        return mx, idx

    gl = jnp.where(lane < n_groups, lg, ninf)
    gmax, gidx = top1(gl)
    p_g = 1.0 / jnp.sum(jnp.exp(gl - gmax), axis=-1, keepdims=True)
    lo = n_groups + per_group * gidx
    el = jnp.where((lane >= lo) & (lane < lo + per_group), lg, ninf)
    m1, i1 = top1(el)
    m2, i2 = top1(jnp.where(lane == i1, ninf, el))
    e21 = jnp.exp(m2 - m1)
    g1 = p_g / (1.0 + e21)
    g2 = g1 * e21
    eid_ref[...] = jnp.where(lane == 0, i1 - n_groups, jnp.where(lane == 1, i2 - n_groups, 0))
    gate_ref[...] = jnp.where(lane == 0, g1, jnp.where(lane == 1, g2, 0.0))


def _router(h, n, g, w_r, b_r, n_groups, per_group, tm):
    d = h.shape[1]
    row = lambda w: pl.BlockSpec((tm, w), lambda i: (i, 0))
    return pl.pallas_call(
        functools.partial(_router_kernel, n_groups=n_groups, per_group=per_group),
        out_shape=[jax.ShapeDtypeStruct((n, LANES), jnp.int32), jax.ShapeDtypeStruct((n, LANES), F32),
                   jax.ShapeDtypeStruct((n, d // 2), U32)],
        grid=(n // tm,),
        in_specs=[row(d),
                  pl.BlockSpec((1, d), lambda i: (0, 0)),
                  pl.BlockSpec((d, LANES), lambda i: (0, 0)),
                  pl.BlockSpec((1, LANES), lambda i: (0, 0))],
        out_specs=[row(LANES), row(LANES), row(d // 2)],
        compiler_params=_params("parallel"),
        name="router",
    )(h, g.reshape(1, d), w_r, b_r)


def _dispatch_tables(e_id, n_exp):
    n, k = e_id.shape
    m = n * k
    sb = MOE_SUB
    rc = sb * MOE_SUBS_PER_CHUNK
    n_chunks = _cdiv(m, rc) + n_exp
    n_sub_max = _cdiv(m, sb) + n_exp
    e_flat = e_id.reshape(m)
    onehot = (e_flat[:, None] == jnp.arange(n_exp, dtype=jnp.int32)[None, :]).astype(jnp.int32)
    csum = jnp.cumsum(onehot, axis=0)
    counts = csum[-1]
    rank = jnp.take_along_axis(csum, e_flat[:, None], axis=1)[:, 0] - 1
    cpe = (counts + rc - 1) // rc
    chunk_end = jnp.cumsum(cpe)
    chunk_start = chunk_end - cpe
    n_used = chunk_end[-1]
    dest = (chunk_start[e_flat] * rc + rank).astype(jnp.int32)
    c_ar = jnp.arange(n_chunks, dtype=jnp.int32)
    used = c_ar < n_used
    c_cl = jnp.minimum(c_ar, n_used - 1)
    chunk_e = jnp.minimum(jnp.searchsorted(chunk_end, c_cl, side='right'), n_exp - 1).astype(jnp.int32)
    rows = jnp.clip(counts[chunk_e] - (c_cl - chunk_start[chunk_e]) * rc, 0, rc)
    nsub = jnp.where(used, (rows + sb - 1) // sb, 0).astype(jnp.int32)
    in_blk = c_cl.astype(jnp.int32)
    out_blk = jnp.where(used, c_ar, n_chunks).astype(jnp.int32)
    tok = (jnp.arange(m, dtype=jnp.int32) // k)
    src = jnp.zeros((n_chunks * rc,), jnp.int32).at[dest].set(tok)
    sub_active = (jnp.arange(MOE_SUBS_PER_CHUNK, dtype=jnp.int32)[None, :] < nsub[:, None]).reshape(-1)
    n_act = jnp.sum(sub_active.astype(jnp.int32))
    sub_ids = jnp.nonzero(sub_active, size=n_sub_max, fill_value=0)[0].astype(jnp.int32)
    sub_ids = jnp.where(jnp.arange(n_sub_max) < n_act, sub_ids, sub_ids[n_act - 1])
    src_sub = src.reshape(n_chunks * MOE_SUBS_PER_CHUNK, sb)[sub_ids].reshape(-1)
    rows_sub = jnp.clip(rows[:, None] - jnp.arange(MOE_SUBS_PER_CHUNK, dtype=jnp.int32)[None, :] * sb, 0, sb)
    grp_sub = ((rows_sub.reshape(-1)[sub_ids] + GATHER_GROUP - 1) // GATHER_GROUP).astype(jnp.int32)
    return dict(dest=dest, chunk_e=chunk_e, nsub=nsub, in_blk=in_blk, out_blk=out_blk,
                sub_ids=sub_ids, src_sub=src_sub, grp_sub=grp_sub, n_act=n_act.reshape(1).astype(jnp.int32),
                n_chunks=n_chunks, n_sub_max=n_sub_max)


def _gather_kernel(src_ref, sub_ref, nact_ref, grp_ref, xp_hbm, o_ref, buf, sem, *, sb):
    i = pl.program_id(0)
    n_act = nact_ref[0]
    half = xp_hbm.shape[1]

    def issue(step, slot):
        def body(it, carry):
            for u in range(DMA_UNROLL):
                r = it * DMA_UNROLL + u
                tok = src_ref[step * sb + r]
                pltpu.make_async_copy(xp_hbm.at[pl.ds(tok, 1)], buf.at[slot, pl.ds(r, 1)], sem.at[slot]).start()
            return carry
        lax.fori_loop(0, grp_ref[step] * (GATHER_GROUP // DMA_UNROLL), body, 0)

    @pl.when(i == 0)
    def _():
        buf[...] = jnp.zeros(buf.shape, buf.dtype)
        issue(0, 0)

    @pl.when(i + 1 < n_act)
    def _():
        issue(i + 1, (i + 1) % 2)

    @pl.when(i < n_act)
    def _():
        slot = i % 2

        def wait_group(gi, carry):
            pltpu.make_async_copy(xp_hbm.at[pl.ds(0, GATHER_GROUP)], buf.at[slot, pl.ds(0, GATHER_GROUP)],
                                  sem.at[slot]).wait()
            return carry
        lax.fori_loop(0, grp_ref[i], wait_group, 0)
        rows = 32
        for c in range(sb // rows):
            u = buf[slot, c * rows:(c + 1) * rows, :]
            hi = pltpu.bitcast(u & jnp.uint32(0xFFFF0000), F32)
            lo = pltpu.bitcast(u << 16, F32)
            o_ref[c * rows:(c + 1) * rows, 0:half] = hi.astype(o_ref.dtype)
            o_ref[c * rows:(c + 1) * rows, half:2 * half] = lo.astype(o_ref.dtype)


def _gather_rows(xp, tabs):
    n, half = xp.shape
    sb = MOE_SUB
    rows = tabs['n_chunks'] * sb * MOE_SUBS_PER_CHUNK
    return pl.pallas_call(
        functools.partial(_gather_kernel, sb=sb),
        out_shape=jax.ShapeDtypeStruct((rows, 2 * half), BF16),
        grid_spec=pltpu.PrefetchScalarGridSpec(
            num_scalar_prefetch=4,
            grid=(tabs['n_sub_max'],),
            in_specs=[pl.BlockSpec(memory_space=pl.ANY)],
            out_specs=pl.BlockSpec((sb, 2 * half), lambda i, src, sub, nact, grp: (sub[i], 0)),
            scratch_shapes=[pltpu.VMEM((2, sb, half), U32), pltpu.SemaphoreType.DMA((2,))]),
        compiler_params=_params("arbitrary"),
        name="moe_gather",
    )(tabs['src_sub'], tabs['sub_ids'], tabs['n_act'], tabs['grp_sub'], xp)


def _moe_kernel(cea_ref, nsa_ref, iba_ref, ceb_ref, nsb_ref, obb_ref, x_ref, wg_ref, wu_ref, wd_ref, o_ref, hid_sc,
                *, sb, n_t, tf):
    g = pl.program_id(0)
    k = pl.program_id(1)
    nsa = nsa_ref[g]
    nsb = nsb_ref[g]
    cur = g % 2

    for r in range(1, MOE_SUBS_PER_CHUNK + 1):
        rows = r * sb

        @pl.when(nsb == r)
        def _():
            o_ref[0:rows, :] = jnp.dot(hid_sc[1 - cur, 0:rows, :], wd_ref[0].astype(BF16),
                                       preferred_element_type=F32)

        @pl.when(nsa == r)
        def _():
            x = x_ref[0:rows, :]
            a = jnp.dot(x, wg_ref[0].astype(BF16), preferred_element_type=F32)
            u = jnp.dot(x, wu_ref[0].astype(BF16), preferred_element_type=F32)
            hid = (a * jax.nn.sigmoid(a) * u).astype(BF16)
            for f in range(n_t):
                @pl.when(k == f)
                def _():
                    hid_sc[cur, 0:rows, f * tf:(f + 1) * tf] = hid


def _moe_mlp(x_pad, w_gate, w_up, w_down, tabs):
    n_exp, d, ff = w_gate.shape
    sb = MOE_SUB
    rc = sb * MOE_SUBS_PER_CHUNK
    n_t = 4
    tf = ff // n_t
    td = d // n_t
    n_chunks = tabs['n_chunks']
    i32 = lambda v: jnp.asarray(v, jnp.int32).reshape(1)
    ce, ns, ib, ob = tabs['chunk_e'], tabs['nsub'], tabs['in_blk'], tabs['out_blk']
    ce_a = jnp.concatenate([ce, ce[-1:]])
    ns_a = jnp.concatenate([ns, i32(0)])
    ib_a = jnp.concatenate([ib, ib[-1:]])
    ce_b = jnp.concatenate([ce[:1], ce])
    ns_b = jnp.concatenate([i32(0), ns])
    ob_b = jnp.concatenate([i32(n_chunks + 1), ob])
    tile = lambda n, g, k: jnp.where(n[g] > 0, k, n_t - 1)
    wmap = lambda g, k, cea, nsa, iba, ceb, nsb, obb: (cea[g], 0, tile(nsa, g, k))
    return pl.pallas_call(
        functools.partial(_moe_kernel, sb=sb, n_t=n_t, tf=tf),
        out_shape=jax.ShapeDtypeStruct(((n_chunks + 2) * rc, d), F32),
        grid_spec=pltpu.PrefetchScalarGridSpec(
            num_scalar_prefetch=6,
            grid=(n_chunks + 1, n_t),
            in_specs=[pl.BlockSpec((rc, d), lambda g, k, cea, nsa, iba, ceb, nsb, obb: (iba[g], 0)),
                      pl.BlockSpec((1, d, tf), wmap),
                      pl.BlockSpec((1, d, tf), wmap),
                      pl.BlockSpec((1, ff, td), lambda g, k, cea, nsa, iba, ceb, nsb, obb:
                                   (ceb[g], 0, tile(nsb, g, k)))],
            out_specs=pl.BlockSpec((rc, td), lambda g, k, cea, nsa, iba, ceb, nsb, obb:
                                   (obb[g], jnp.where(nsb[g] > 0, k, 0))),
            scratch_shapes=[pltpu.VMEM((2, rc, ff), BF16)]),
        compiler_params=_params("arbitrary", "arbitrary"),
        name="moe_mlp",
    )(ce_a, ns_a, ib_a, ce_b, ns_b, ob_b, x_pad, w_gate, w_up, w_down)


def _combine_kernel(dest_ref, row0_ref, y_hbm, h_hbm, gate_hbm, o_ref, ybuf, hbuf, gbuf, sem, *, tm):
    i = pl.program_id(0)
    n = pl.num_programs(0)

    def contiguous(step, slot):
        r0 = pl.multiple_of(row0_ref[step], SUBLANES)
        return (pltpu.make_async_copy(h_hbm.at[pl.ds(r0, tm)], hbuf.at[slot], sem.at[slot]),
                pltpu.make_async_copy(gate_hbm.at[pl.ds(r0, tm)], gbuf.at[slot], sem.at[slot]))

    def issue(step, slot):
        for cp in contiguous(step, slot):
            cp.start()
        base = row0_ref[step] * TOP_K

        def body(it, carry):
            for u in range(DMA_UNROLL // TOP_K):
                r = it * (DMA_UNROLL // TOP_K) + u
                for k in range(TOP_K):
                    row = dest_ref[base + r * TOP_K + k]
                    pltpu.make_async_copy(y_hbm.at[pl.ds(row, 1)], ybuf.at[slot, k, pl.ds(r, 1)],
                                          sem.at[slot]).start()
            return carry
        lax.fori_loop(0, tm * TOP_K // DMA_UNROLL, body, 0)

    @pl.when(i == 0)
    def _():
        issue(0, 0)

    @pl.when(i + 1 < n)
    def _():
        issue(i + 1, (i + 1) % 2)

    slot = i % 2
    for cp in contiguous(i, slot):
        cp.wait()
    for k in range(TOP_K):
        pltpu.make_async_copy(y_hbm.at[pl.ds(0, tm)], ybuf.at[slot, k], sem.at[slot]).wait()
    gate = gbuf[slot]
    out = hbuf[slot]
    for k in range(TOP_K):
        out = out + gate[:, k:k + 1] * ybuf[slot, k]
    o_ref[...] = out


def _combine(y_pad, h, gate, dest, row0, tm):
    d = h.shape[1]
    n_tiles = row0.shape[0]
    return pl.pallas_call(
        functools.partial(_combine_kernel, tm=tm),
        out_shape=jax.ShapeDtypeStruct((n_tiles * tm, d), F32),
        grid_spec=pltpu.PrefetchScalarGridSpec(
            num_scalar_prefetch=2,
            grid=(n_tiles,),
            in_specs=[pl.BlockSpec(memory_space=pl.ANY), pl.BlockSpec(memory_space=pl.ANY),
                      pl.BlockSpec(memory_space=pl.ANY)],
            out_specs=pl.BlockSpec((tm, d), lambda i, dest, row0: (i, 0)),
            scratch_shapes=[pltpu.VMEM((2, TOP_K, tm, d), F32), pltpu.VMEM((2, tm, d), F32),
                            pltpu.VMEM((2, tm, LANES), F32), pltpu.SemaphoreType.DMA((2,))]),
        compiler_params=_params("arbitrary"),
        name="moe_combine",
    )(dest, row0, y_pad, h, gate)


def _mixer_inputs(h, norm_g, w_in_bf, wf, bf, q_g, k_g, conv_dim, attn_w, n_heads, tm, q_scale):
    hd = attn_w // n_heads
    tn = 512
    xn = _rmsnorm(h, norm_g, _div_tile(h.shape[0], 256, 8))
    glu = _proj_glu(xn, w_in_bf, conv_dim, tm, tn // 2)
    c1 = 2 * conv_dim
    (q,) = _proj_headnorm(xn, w_in_bf, q_g, c1, attn_w, hd, q_scale, (BF16,), tm, tn)
    k32, k16 = _proj_headnorm(xn, w_in_bf, k_g, c1 + attn_w, attn_w, hd, None, (F32, BF16), tm, tn)
    v32, v16 = _proj_plain(xn, w_in_bf, c1 + 2 * attn_w, attn_w, (F32, BF16), tm, tn)
    logf = _proj_logf(xn, wf, bf, tm)[:, :n_heads]
    return glu, q, k32, k16, v32, v16, logf


def kernel(x_prompt, x_sample, cache_k, cache_v, cache_logf, cache_conv, meta_tokens, norm_mix_g, w_in, b_forget,
           q_norm_g, k_norm_g, conv_w, conv_b, conv_ln_g, conv_ln_b, w_out, norm_ffn_g, w_router_group,
           b_router_group, w_router_expert, b_router_expert, w_gate, w_up, w_down):
    depth = w_in.shape[0]
    assert depth == 1
    bp, seq, d = x_prompt.shape
    bs, tq, _ = x_sample.shape
    n_heads = cache_k.shape[3]
    hd = cache_k.shape[4]
    attn_w = n_heads * hd
    conv_dim = conv_w.shape[2]
    hist_len = conv_w.shape[1] - 1
    past = cache_k.shape[2]
    n_groups = w_router_group.shape[2]
    n_exp = w_router_expert.shape[2]
    per_group = n_exp // n_groups
    assert n_groups + n_exp <= LANES and 2 * conv_dim % 512 == 0 and attn_w % 512 == 0
    t = N_META + seq
    npr = bp * t
    ns = bs * tq
    n_all = npr + ns

    w_in_bf = w_in[0].astype(BF16)
    c4 = 2 * conv_dim + 3 * attn_w
    wf = jnp.pad(w_in[0][:, c4:], ((0, 0), (0, LANES - n_heads))).astype(BF16)
    bf = jnp.pad(b_forget[0], (0, LANES - n_heads)).reshape(1, LANES)
    w_out_bf = w_out[0].astype(BF16)
    w_r = jnp.pad(jnp.concatenate([w_router_group[0], w_router_expert[0]], axis=1),
                  ((0, 0), (0, LANES - n_groups - n_exp))).astype(BF16)
    b_r = jnp.pad(jnp.concatenate([b_router_group[0], b_router_expert[0]]),
                  (0, LANES - n_groups - n_exp)).reshape(1, LANES)

    h_p = jnp.concatenate([jnp.broadcast_to(meta_tokens[None], (bp, N_META, d)), x_prompt], axis=1).reshape(npr, d)
    h_s = x_sample.reshape(ns, d)

    tm_p = _div_tile(npr, 1400, 16)
    tm_s = _div_tile(ns, 1400, 16)
    mix = functools.partial(_mixer_inputs, norm_g=norm_mix_g[0], w_in_bf=w_in_bf, wf=wf, bf=bf, q_g=q_norm_g[0],
                            k_g=k_norm_g[0], conv_dim=conv_dim, attn_w=attn_w, n_heads=n_heads)
    glu_p, q_p, k32_p, k16_p, v32_p, v16_p, logf_p = mix(h_p, tm=tm_p, q_scale=hd ** -0.5 * LOG2E)
    glu_s, q_s, k32_s, k16_s, v32_s, v16_s, logf_s = mix(h_s, tm=tm_s, q_scale=hd ** -0.5)

    glu_p3 = glu_p.reshape(bp, t, conv_dim)
    glu_s3 = glu_s.reshape(bs, tq, conv_dim)
    hist_p = jnp.zeros((bp, hist_len + 2, conv_dim), F32)
    hist_s = jnp.pad(cache_conv[0], ((0, 0), (2, 0), (0, 0)))
    conv_args = (conv_w[0], conv_b[0], conv_ln_g[0], conv_ln_b[0])
    conv_p = _conv_module(glu_p3, hist_p, *conv_args)
    conv_s = _conv_module(glu_s3, hist_s, *conv_args)

    logf_p3 = logf_p.reshape(bp, t, n_heads)
    logf_s3 = logf_s.reshape(bs, tq, n_heads)
    cblk = 256
    pad_t = lambda a: jnp.pad(a, ((0, 0), (0, _cdiv(a.shape[1], cblk) * cblk - a.shape[1]), (0, 0)))
    cum_p = _cumsum_time(pad_t(logf_p3), cblk)[:, :t]
    lf_all = jnp.concatenate([cache_logf[0], logf_s3], axis=1)
    cum_s = _cumsum_time(pad_t(lf_all), cblk)[:, :past + tq]
    r3 = lambda a, b_, t_: a.reshape(b_, t_, attn_w)
    attn_p = _attn_prompt(r3(q_p, bp, t), r3(k16_p, bp, t), r3(v16_p, bp, t), cum_p, n_heads)
    attn_s = _attn_sample(r3(q_s, bs, tq), cache_k[0].reshape(bs, past, attn_w), cache_v[0].reshape(bs, past, attn_w),
                          r3(k16_s, bs, tq), r3(v16_s, bs, tq), cum_s, n_heads)

    tn_o = 512
    h2 = _outproj(attn_p.reshape(npr, attn_w), conv_p.reshape(npr, conv_dim), h_p,
                  attn_s.reshape(ns, attn_w), conv_s.reshape(ns, conv_dim), h_s, w_out_bf, tm_p, tn_o)

    tm_r = _div_tile(n_all, 128, 8)
    eid, gate, xp = _router(h2, n_all, norm_ffn_g[0], w_r, b_r, n_groups, per_group, tm_r)
    tabs = _dispatch_tables(eid[:, :TOP_K], n_exp)
    x_pad = _gather_rows(xp, tabs)
    y_pad = _moe_mlp(x_pad, w_gate[0], w_up[0], w_down[0], tabs)

    tm_cp = _div_tile(seq, 128, 8)
    tiles_b = seq // tm_cp
    i_p = jnp.arange(bp * tiles_b, dtype=jnp.int32)
    row0_p = (i_p // tiles_b) * t + N_META + (i_p % tiles_b) * tm_cp
    tm_cs = _div_tile(ns, 128, 8)
    row0_s = npr + jnp.arange(ns // tm_cs, dtype=jnp.int32) * tm_cs
    y_p = _combine(y_pad, h2, gate, tabs['dest'], row0_p, tm_cp).reshape(bp, seq, d)
    y_s = _combine(y_pad, h2, gate, tabs['dest'], row0_s, tm_cs).reshape(bs, tq, d)

    st = lambda a, b_, t_: a.reshape(1, b_, t_, n_heads, hd)
    return (y_p, y_s,
            st(k32_p, bp, t), st(v32_p, bp, t), logf_p3[None], glu_p3[:, t - hist_len:][None],
            st(k32_s, bs, tq), st(v32_s, bs, tq), logf_s3[None],
            jnp.concatenate([cache_conv[0], glu_s3], axis=1)[:, -hist_len:][None])
```

```python
import functools

import jax
import jax.numpy as jnp
from jax import lax
from jax.experimental import pallas as pl
from jax.experimental.pallas import tpu as pltpu

F32 = jnp.float32
BF16 = jnp.bfloat16
U32 = jnp.uint32

N_META = 16
CONV_GROUPS = 16
TOP_K = 2
EPS = 1e-6
NEG_INF = -1e30

LANES = 128
SUBLANES = 8
ATTN_BLOCK = 512
MOE_SUB = 128
MOE_SUBS_PER_CHUNK = 6
DMA_UNROLL = 8
GATHER_GROUP = 32
LOG2E = 1.4426950408889634
BIAS_PARTS = 3
VMEM_LIMIT = 56 * 1024 * 1024


def _cdiv(a, b):
    return -(-a // b)


def _gcd(a, b):
    while b:
        a, b = b, a % b
    return a


def _div_tile(n, target, mult):
    best = None
    for d in range(mult, min(n, target) + 1, mult):
        if n % d == 0:
            best = d
    assert best is not None, (n, target, mult)
    return best


def _params(*sem):
    return pltpu.CompilerParams(dimension_semantics=sem, vmem_limit_bytes=VMEM_LIMIT)


def _rmsnorm_kernel(x_ref, g_ref, o_ref):
    x = x_ref[...]
    ms = jnp.mean(x * x, axis=-1, keepdims=True)
    o_ref[...] = (x * lax.rsqrt(ms + EPS) * g_ref[...]).astype(o_ref.dtype)


def _rmsnorm(x, g, tm):
    n, d = x.shape
    return pl.pallas_call(
        _rmsnorm_kernel,
        out_shape=jax.ShapeDtypeStruct((n, d), BF16),
        grid=(n // tm,),
        in_specs=[pl.BlockSpec((tm, d), lambda i: (i, 0)), pl.BlockSpec((1, d), lambda i: (0, 0))],
        out_specs=pl.BlockSpec((tm, d), lambda i: (i, 0)),
        compiler_params=_params("parallel"),
        name="rmsnorm",
    )(x, g.reshape(1, d))


def _glu_kernel(x_ref, wa_ref, wb_ref, o_ref):
    x = x_ref[...]
    a = jnp.dot(x, wa_ref[...], preferred_element_type=F32)
    b = jnp.dot(x, wb_ref[...], preferred_element_type=F32)
    o_ref[...] = a * jax.nn.sigmoid(b)


def _proj_glu(xn, w, conv_dim, tm, tn):
    n, d = xn.shape
    nb = conv_dim // tn
    return pl.pallas_call(
        _glu_kernel,
        out_shape=jax.ShapeDtypeStruct((n, conv_dim), F32),
        grid=(n // tm, nb),
        in_specs=[pl.BlockSpec((tm, d), lambda i, j: (i, 0)),
                  pl.BlockSpec((d, tn), lambda i, j: (0, j)),
                  pl.BlockSpec((d, tn), lambda i, j: (0, nb + j))],
        out_specs=pl.BlockSpec((tm, tn), lambda i, j: (i, j)),
        compiler_params=_params("parallel", "arbitrary"),
        name="proj_glu",
    )(xn, w, w)


def _headnorm_kernel(x_ref, w_ref, g_ref, *o_refs, scale, hd):
    z = jnp.dot(x_ref[...], w_ref[...], preferred_element_type=F32)
    g = g_ref[...]
    for h in range(z.shape[1] // hd):
        zh = z[:, h * hd:(h + 1) * hd]
        y = zh * lax.rsqrt(jnp.mean(zh * zh, axis=-1, keepdims=True) + EPS) * g
        if scale is not None:
            y = y * scale
        for o_ref in o_refs:
            o_ref[:, h * hd:(h + 1) * hd] = y.astype(o_ref.dtype)


def _proj_headnorm(xn, w, g, col0, width, hd, scale, out_dtypes, tm, tn):
    n, d = xn.shape
    nb = width // tn
    b0 = col0 // tn
    outs = pl.pallas_call(
        functools.partial(_headnorm_kernel, scale=scale, hd=hd),
        out_shape=[jax.ShapeDtypeStruct((n, width), dt) for dt in out_dtypes],
        grid=(n // tm, nb),
        in_specs=[pl.BlockSpec((tm, d), lambda i, j: (i, 0)),
                  pl.BlockSpec((d, tn), lambda i, j: (0, b0 + j)),
                  pl.BlockSpec((1, hd), lambda i, j: (0, 0))],
        out_specs=[pl.BlockSpec((tm, tn), lambda i, j: (i, j)) for _ in out_dtypes],
        compiler_params=_params("parallel", "arbitrary"),
        name="proj_headnorm",
    )(xn, w, g.reshape(1, hd))
    return outs


def _plain_kernel(x_ref, w_ref, *o_refs):
    z = jnp.dot(x_ref[...], w_ref[...], preferred_element_type=F32)
    for o_ref in o_refs:
        o_ref[...] = z.astype(o_ref.dtype)


def _proj_plain(xn, w, col0, width, out_dtypes, tm, tn):
    n, d = xn.shape
    nb = width // tn
    b0 = col0 // tn
    return pl.pallas_call(
        _plain_kernel,
        out_shape=[jax.ShapeDtypeStruct((n, width), dt) for dt in out_dtypes],
        grid=(n // tm, nb),
        in_specs=[pl.BlockSpec((tm, d), lambda i, j: (i, 0)),
                  pl.BlockSpec((d, tn), lambda i, j: (0, b0 + j))],
        out_specs=[pl.BlockSpec((tm, tn), lambda i, j: (i, j)) for _ in out_dtypes],
        compiler_params=_params("parallel", "arbitrary"),
        name="proj_plain",
    )(xn, w)


def _logf_kernel(x_ref, w_ref, b_ref, o_ref):
    z = jnp.dot(x_ref[...], w_ref[...], preferred_element_type=F32) + b_ref[...]
    o_ref[...] = jnp.minimum(z, 0.0) - jnp.log1p(jnp.exp(-jnp.abs(z)))


def _proj_logf(xn, wf, bf, tm):
    n, d = xn.shape
    return pl.pallas_call(
        _logf_kernel,
        out_shape=jax.ShapeDtypeStruct((n, LANES), F32),
        grid=(n // tm,),
        in_specs=[pl.BlockSpec((tm, d), lambda i: (i, 0)),
                  pl.BlockSpec((d, LANES), lambda i: (0, 0)),
                  pl.BlockSpec((1, LANES), lambda i: (0, 0))],
        out_specs=pl.BlockSpec((tm, LANES), lambda i: (i, 0)),
        compiler_params=_params("parallel"),
        name="proj_logf",
    )(xn, wf, bf)


def _conv_kernel(glu_ref, hist_ref, cw_ref, cb_ref, lg_ref, lb_ref, o_ref, full_sc, *, t, tt, width):
    hp = hist_ref.shape[1]
    lanes = glu_ref.shape[2]
    full_sc[0:hp, :] = hist_ref[0]
    full_sc[hp:hp + t, :] = glu_ref[0]
    full_sc[hp + t:hp + t + SUBLANES, :] = jnp.zeros((SUBLANES, lanes), F32)

    def chunk(ci, carry):
        t0 = pl.multiple_of(ci * tt, 16)
        for lt in range(lanes // LANES):
            ls = slice(lt * LANES, (lt + 1) * LANES)
            win = full_sc[pl.ds(t0, tt + hp + SUBLANES), ls]
            y = None
            for r in range(SUBLANES):
                part = None
                for a in range(_cdiv(width + 2, SUBLANES)):
                    j = SUBLANES * a + r
                    if not 2 <= j < width + 2:
                        continue
                    term = win[SUBLANES * a:SUBLANES * a + tt + SUBLANES, :] * cw_ref[j - 2:j - 1, ls]
                    part = term if part is None else part + term
                if part is None:
                    continue
                shifted = part[r:r + tt, :]
                y = shifted if y is None else y + shifted
            y = y + cb_ref[:, ls]
            mu = jnp.mean(y, axis=-1, keepdims=True)
            dlt = y - mu
            var = jnp.mean(dlt * dlt, axis=-1, keepdims=True)
            yn = dlt * lax.rsqrt(var + EPS) * lg_ref[:, ls] + lb_ref[:, ls]
            o_ref[0, pl.ds(t0, tt), ls] = (yn * jax.nn.sigmoid(yn)).astype(o_ref.dtype)
        return carry

    lax.fori_loop(0, t // tt, chunk, 0)


def _conv_module(glu, hist, conv_w, conv_b, ln_g, ln_b):
    b, t, c = glu.shape
    width = conv_w.shape[0]
    hp = hist.shape[1]
    assert hp == 32 and width + 1 == hp and c // CONV_GROUPS == LANES
    tt = _div_tile(t, 64, 16)
    tc = 8 * LANES
    cw = jnp.concatenate([conv_w, jnp.zeros((hp - width, c), F32)], axis=0)
    vec = lambda v: v.reshape(1, c)
    return pl.pallas_call(
        functools.partial(_conv_kernel, t=t, tt=tt, width=width),
        out_shape=jax.ShapeDtypeStruct((b, t, c), BF16),
        grid=(b, c // tc),
        in_specs=[pl.BlockSpec((1, t, tc), lambda i, j: (i, 0, j)),
                  pl.BlockSpec((1, hp, tc), lambda i, j: (i, 0, j)),
                  pl.BlockSpec((hp, tc), lambda i, j: (0, j)),
                  pl.BlockSpec((1, tc), lambda i, j: (0, j)),
                  pl.BlockSpec((1, tc), lambda i, j: (0, j)),
                  pl.BlockSpec((1, tc), lambda i, j: (0, j))],
        out_specs=pl.BlockSpec((1, t, tc), lambda i, j: (i, 0, j)),
        scratch_shapes=[pltpu.VMEM((hp + t + SUBLANES, tc), F32)],
        compiler_params=_params("parallel", "parallel"),
        name="conv_ln_swish",
    )(glu, hist, cw, vec(conv_b), vec(ln_g), vec(ln_b))


def _cumsum_kernel(x_ref, o_ref, *, nblk, blk):
    r = lax.broadcasted_iota(jnp.int32, (blk, blk), 0)
    c = lax.broadcasted_iota(jnp.int32, (blk, blk), 1)
    tri = (c <= r).astype(F32)
    carry = jnp.zeros((1, x_ref.shape[2]), F32)
    for i in range(nblk):
        x = x_ref[0, i * blk:(i + 1) * blk, :]
        o_ref[0, i * blk:(i + 1) * blk, :] = jnp.dot(
            tri, x, precision=lax.Precision.HIGHEST, preferred_element_type=F32) + carry
        carry = carry + jnp.sum(x, axis=0, keepdims=True)


def _cumsum_time(x, blk):
    b, t, h = x.shape
    return pl.pallas_call(
        functools.partial(_cumsum_kernel, nblk=t // blk, blk=blk),
        out_shape=jax.ShapeDtypeStruct((b, t, h), F32),
        grid=(b,),
        in_specs=[pl.BlockSpec((1, t, h), lambda i: (i, 0, 0))],
        out_specs=pl.BlockSpec((1, t, h), lambda i: (i, 0, 0)),
        compiler_params=_params("parallel"),
        name="logf_cumsum",
    )(x)


def _attn_update(q, k, v, cq, ck, mask, m, l, acc):
    s = lax.dot_general(q, k, (((1,), (1,)), ((), ())), preferred_element_type=F32)
    s = s - ck
    if mask is not None:
        s = jnp.where(mask, s, NEG_INF)
    m_new = jnp.maximum(m, jnp.max(s, axis=-1, keepdims=True) + cq)
    alpha = jnp.exp(m - m_new)
    p = jnp.exp(s + (cq - m_new))
    l_new = alpha * l + jnp.sum(p, axis=-1, keepdims=True)
    acc_new = alpha * acc + jnp.dot(p.astype(BF16), v, preferred_element_type=F32)
    return m_new, l_new, acc_new


def _softmax_init(tq, d):
    return (jnp.full((tq, 1), NEG_INF, F32), jnp.zeros((tq, 1), F32), jnp.zeros((tq, d), F32))


def _head_column(c, h):
    sel = lax.broadcasted_iota(jnp.int32, c.shape, 1) == h
    return jnp.sum(jnp.where(sel, c, 0.0), axis=-1, keepdims=True)


def _causal_mask(n):
    r = lax.broadcasted_iota(jnp.int32, (n, n), 0)
    c = lax.broadcasted_iota(jnp.int32, (n, n), 1)
    return c <= r


def _attn_prompt_kernel(q_ref, k_ref, v_ref, cum_ref, o_ref, qa_sc, ka_sc, s_sc, d_sc, p_sc, pd_sc, l_sc,
                        *, t, bq, rs):
    h = pl.program_id(1)
    d = q_ref.shape[2]
    nfull = t // bq
    tail = t - nfull * bq
    nt = (((1,), (1,)), ((), ()))

    lane = lax.broadcasted_iota(jnp.int32, (t, d), 1)
    rem = _head_column(cum_ref[0], h) * LOG2E
    aug = jnp.zeros((t, d), F32)
    for j in range(BIAS_PARTS):
        part = rem.astype(BF16).astype(F32)
        aug = jnp.where(lane == j, part, aug)
        rem = rem - part
    ka_sc[:, 0:d] = k_ref[0]
    ka_sc[:, d:2 * d] = aug.astype(BF16)
    qa_sc[:, 0:d] = q_ref[0]
    qa_sc[:, d:2 * d] = jnp.where(lane < BIAS_PARTS, -1.0, 0.0).astype(BF16)

    blocks = [(i * bq, bq) for i in range(nfull)] + ([(nfull * bq, tail)] if tail else [])
    for q0, nq in blocks:
        q = qa_sc[q0:q0 + nq, :]
        if q0:
            s_sc[0:nq, 0:q0] = lax.dot_general(q, ka_sc[0:q0, :], nt, preferred_element_type=F32)
        d_sc[0:nq, 0:nq] = lax.dot_general(q, ka_sc[q0:q0 + nq, :], nt, preferred_element_type=F32)
        rows = min(rs, nq)

        def softmax_rows(ri, carry, q0=q0, nq=nq, rows=rows):
            r0 = ri * rows
            cq = _head_column(cum_ref[0, pl.ds(q0 + r0, rows), :], h) * LOG2E
            rr = lax.broadcasted_iota(jnp.int32, (rows, nq), 0) + r0
            cc = lax.broadcasted_iota(jnp.int32, (rows, nq), 1)
            sd = jnp.where(cc <= rr, d_sc[pl.ds(r0, rows), 0:nq], NEG_INF)
            mx = jnp.max(sd, axis=-1, keepdims=True)
            if q0:
                sp = s_sc[pl.ds(r0, rows), 0:q0]
                mx = jnp.maximum(mx, jnp.max(sp, axis=-1, keepdims=True))
            shift = cq - (mx + cq)
            pd = jnp.exp2(sd + shift)
            l = jnp.sum(pd, axis=-1, keepdims=True)
            pd_sc[pl.ds(r0, rows), 0:nq] = pd.astype(BF16)
            if q0:
                pp = jnp.exp2(sp + shift)
                l = l + jnp.sum(pp, axis=-1, keepdims=True)
                p_sc[pl.ds(r0, rows), 0:q0] = pp.astype(BF16)
            l_sc[pl.ds(r0, rows), :] = l
            return carry

        for ri in range(nq // rows):
            softmax_rows(ri, 0)
        o = jnp.dot(pd_sc[0:nq, 0:nq], v_ref[0, q0:q0 + nq, :], preferred_element_type=F32)
        if q0:
            o = o + jnp.dot(p_sc[0:nq, 0:q0], v_ref[0, 0:q0, :], preferred_element_type=F32)
        o_ref[0, q0:q0 + nq, :] = (o / l_sc[0:nq, :]).astype(o_ref.dtype)


def _blocked_rows(cum, blk):
    b, t, h = cum.shape
    nblk = _cdiv(t, blk)
    c = jnp.pad(cum, ((0, 0), (0, nblk * blk - t), (0, 0)))
    return c.transpose(0, 2, 1).reshape(b, h, nblk, 1, blk)


def _attn_prompt(q, k, v, cum, n_heads):
    b, t, w = q.shape
    d = w // n_heads
    bq = ATTN_BLOCK
    nfull = t // bq
    assert nfull >= 1 and (t % bq) % 16 == 0
    blk = lambda: pl.BlockSpec((1, t, d), lambda i, j: (i, 0, j))
    return pl.pallas_call(
        functools.partial(_attn_prompt_kernel, t=t, bq=bq, rs=128),
        out_shape=jax.ShapeDtypeStruct((b, t, w), BF16),
        grid=(b, n_heads),
        in_specs=[blk(), blk(), blk(),
                  pl.BlockSpec((1, t, n_heads), lambda i, j: (i, 0, 0))],
        out_specs=blk(),
        scratch_shapes=[pltpu.VMEM((t, 2 * d), BF16), pltpu.VMEM((t, 2 * d), BF16),
                        pltpu.VMEM((bq, nfull * bq), F32), pltpu.VMEM((bq, bq), F32),
                        pltpu.VMEM((bq, nfull * bq), BF16), pltpu.VMEM((bq, bq), BF16),
                        pltpu.VMEM((bq, 1), F32)],
        compiler_params=_params("parallel", "parallel"),
        name="fox_attn_prompt",
    )(q, k, v, cum)


def _attn_sample_kernel(q_ref, ck_ref, cv_ref, k_ref, v_ref, cum_ref, cumt_ref, o_ref, *, past, bk, hpb, d):
    tq = q_ref.shape[1]
    for hh in range(hpb):
        cols = slice(hh * d, (hh + 1) * d)
        q = q_ref[0, :, cols]
        cq = _head_column(cum_ref[0], pl.program_id(1) * hpb + hh)
        carry = _softmax_init(tq, d)
        for kj in range(past // bk):
            kb = ck_ref[0, kj * bk:(kj + 1) * bk, cols].astype(BF16)
            vb = cv_ref[0, kj * bk:(kj + 1) * bk, cols].astype(BF16)
            carry = _attn_update(q, kb, vb, cq, cumt_ref[0, hh, kj], None, *carry)
        carry = _attn_update(q, k_ref[0, :, cols], v_ref[0, :, cols], cq, cumt_ref[0, hh, past // bk][:, :tq],
                             _causal_mask(tq), *carry)
        _, l, acc = carry
        o_ref[0, :, cols] = (acc / l).astype(o_ref.dtype)


def _attn_sample(q, cache_k, cache_v, k, v, cum, n_heads):
    b, tq, w = q.shape
    d = w // n_heads
    past = cache_k.shape[1]
    bk = _div_tile(past, ATTN_BLOCK, LANES)
    assert tq <= bk
    cumt = _blocked_rows(cum, bk)
    nblk = cumt.shape[2]
    cum_q = cum[:, past:]
    hpb = 4 if n_heads % 4 == 0 else 1
    new = lambda: pl.BlockSpec((1, tq, hpb * d), lambda i, j: (i, 0, j))
    old = lambda: pl.BlockSpec((1, past, hpb * d), lambda i, j: (i, 0, j))
    return pl.pallas_call(
        functools.partial(_attn_sample_kernel, past=past, bk=bk, hpb=hpb, d=d),
        out_shape=jax.ShapeDtypeStruct((b, tq, w), BF16),
        grid=(b, n_heads // hpb),
        in_specs=[new(), old(), old(), new(), new(),
                  pl.BlockSpec((1, tq, n_heads), lambda i, j: (i, 0, 0)),
                  pl.BlockSpec((1, hpb, nblk, 1, bk), lambda i, j: (i, j, 0, 0, 0))],
        out_specs=new(),
        compiler_params=_params("parallel", "parallel"),
        name="fox_attn_sample",
    )(q, cache_k, cache_v, k, v, cum_q, cumt)


def _outproj_kernel(ap_ref, cp_ref, hp_ref, as_ref, cs_ref, hs_ref, wa_ref, wc_ref, o_ref, *, n_p):
    i = pl.program_id(0)

    def project(a_ref, c_ref, h_ref):
        return (h_ref[...]
                + jnp.dot(a_ref[...], wa_ref[...], preferred_element_type=F32)
                + jnp.dot(c_ref[...], wc_ref[...], preferred_element_type=F32))

    @pl.when(i < n_p)
    def _():
        o_ref[...] = project(ap_ref, cp_ref, hp_ref)

    @pl.when(i == n_p)
    def _():
        o_ref[0:hs_ref.shape[0], :] = project(as_ref, cs_ref, hs_ref)


def _outproj(attn_p, conv_p, h_p, attn_s, conv_s, h_s, w_out, tm, tn):
    npr, wa = attn_p.shape
    ns = attn_s.shape[0]
    wc = conv_p.shape[1]
    d = w_out.shape[1]
    assert wa == wc and npr % tm == 0 and ns <= tm
    n_p = npr // tm
    last = lambda i: jnp.minimum(i, n_p - 1)
    return pl.pallas_call(
        functools.partial(_outproj_kernel, n_p=n_p),
        out_shape=jax.ShapeDtypeStruct(((n_p + 1) * tm, d), F32),
        grid=(n_p + 1, d // tn),
        in_specs=[pl.BlockSpec((tm, wa), lambda i, j: (last(i), 0)),
                  pl.BlockSpec((tm, wc), lambda i, j: (last(i), 0)),
                  pl.BlockSpec((tm, tn), lambda i, j: (last(i), j)),
                  pl.BlockSpec((ns, wa), lambda i, j: (0, 0)),
                  pl.BlockSpec((ns, wc), lambda i, j: (0, 0)),
                  pl.BlockSpec((ns, tn), lambda i, j: (0, j)),
                  pl.BlockSpec((wa, tn), lambda i, j: (0, j)),
                  pl.BlockSpec((wc, tn), lambda i, j: (1, j))],
        out_specs=pl.BlockSpec((tm, tn), lambda i, j: (i, j)),
        compiler_params=_params("parallel", "arbitrary"),
        name="outproj_residual",
    )(attn_p, conv_p, h_p, attn_s, conv_s, h_s, w_out, w_out)


def _router_kernel(h_ref, g_ref, w_ref, b_ref, eid_ref, gate_ref, xp_ref, *, n_groups, per_group):
    x = h_ref[...]
    xn = (x * lax.rsqrt(jnp.mean(x * x, axis=-1, keepdims=True) + EPS) * g_ref[...]).astype(BF16)
    half = x.shape[1] // 2
    bits = pltpu.bitcast(xn.astype(F32), U32)
    xp_ref[...] = bits[:, :half] | (bits[:, half:] >> 16)

    lg = jnp.dot(xn, w_ref[...], preferred_element_type=F32) + b_ref[...]
    lane = lax.broadcasted_iota(jnp.int32, lg.shape, 1)
    ninf = -jnp.inf

    def top1(vals):
        mx = jnp.max(vals, axis=-1, keepdims=True)
        idx = jnp.min(jnp.where(vals == mx, lane, LANES), axis=-1, keepdims=True)
        return mx, idx

    gl = jnp.where(lane < n_groups, lg, ninf)
    gmax, gidx = top1(gl)
    p_g = 1.0 / jnp.sum(jnp.exp(gl - gmax), axis=-1, keepdims=True)
    lo = n_groups + per_group * gidx
    el = jnp.where((lane >= lo) & (lane < lo + per_group), lg, ninf)
    m1, i1 = top1(el)
    m2, i2 = top1(jnp.where(lane == i1, ninf, el))
    e21 = jnp.exp(m2 - m1)
    g1 = p_g / (1.0 + e21)
    g2 = g1 * e21
    eid_ref[...] = jnp.where(lane == 0, i1 - n_groups, jnp.where(lane == 1, i2 - n_groups, 0))
    gate_ref[...] = jnp.where(lane == 0, g1, jnp.where(lane == 1, g2, 0.0))


def _router(h, n, g, w_r, b_r, n_groups, per_group, tm):
    d = h.shape[1]
    row = lambda w: pl.BlockSpec((tm, w), lambda i: (i, 0))
    return pl.pallas_call(
        functools.partial(_router_kernel, n_groups=n_groups, per_group=per_group),
        out_shape=[jax.ShapeDtypeStruct((n, LANES), jnp.int32), jax.ShapeDtypeStruct((n, LANES), F32),
                   jax.ShapeDtypeStruct((n, d // 2), U32)],
        grid=(n // tm,),
        in_specs=[row(d),
                  pl.BlockSpec((1, d), lambda i: (0, 0)),
                  pl.BlockSpec((d, LANES), lambda i: (0, 0)),
                  pl.BlockSpec((1, LANES), lambda i: (0, 0))],
        out_specs=[row(LANES), row(LANES), row(d // 2)],
        compiler_params=_params("parallel"),
        name="router",
    )(h, g.reshape(1, d), w_r, b_r)


def _dispatch_tables(e_id, n_exp):
    n, k = e_id.shape
    m = n * k
    sb = MOE_SUB
    rc = sb * MOE_SUBS_PER_CHUNK
    n_chunks = _cdiv(m, rc) + n_exp
    n_sub_max = _cdiv(m, sb) + n_exp
    e_flat = e_id.reshape(m)
    onehot = (e_flat[:, None] == jnp.arange(n_exp, dtype=jnp.int32)[None, :]).astype(jnp.int32)
    csum = jnp.cumsum(onehot, axis=0)
    counts = csum[-1]
    rank = jnp.take_along_axis(csum, e_flat[:, None], axis=1)[:, 0] - 1
    cpe = (counts + rc - 1) // rc
    chunk_end = jnp.cumsum(cpe)
    chunk_start = chunk_end - cpe
    n_used = chunk_end[-1]
    dest = (chunk_start[e_flat] * rc + rank).astype(jnp.int32)
    c_ar = jnp.arange(n_chunks, dtype=jnp.int32)
    used = c_ar < n_used
    c_cl = jnp.minimum(c_ar, n_used - 1)
    chunk_e = jnp.minimum(jnp.searchsorted(chunk_end, c_cl, side='right'), n_exp - 1).astype(jnp.int32)
    rows = jnp.clip(counts[chunk_e] - (c_cl - chunk_start[chunk_e]) * rc, 0, rc)
    nsub = jnp.where(used, (rows + sb - 1) // sb, 0).astype(jnp.int32)
    in_blk = c_cl.astype(jnp.int32)
    out_blk = jnp.where(used, c_ar, n_chunks).astype(jnp.int32)
    tok = (jnp.arange(m, dtype=jnp.int32) // k)
    src = jnp.zeros((n_chunks * rc,), jnp.int32).at[dest].set(tok)
    sub_active = (jnp.arange(MOE_SUBS_PER_CHUNK, dtype=jnp.int32)[None, :] < nsub[:, None]).reshape(-1)
    n_act = jnp.sum(sub_active.astype(jnp.int32))
    sub_ids = jnp.nonzero(sub_active, size=n_sub_max, fill_value=0)[0].astype(jnp.int32)
    sub_ids = jnp.where(jnp.arange(n_sub_max) < n_act, sub_ids, sub_ids[n_act - 1])
    src_sub = src.reshape(n_chunks * MOE_SUBS_PER_CHUNK, sb)[sub_ids].reshape(-1)
    rows_sub = jnp.clip(rows[:, None] - jnp.arange(MOE_SUBS_PER_CHUNK, dtype=jnp.int32)[None, :] * sb, 0, sb)
    grp_sub = ((rows_sub.reshape(-1)[sub_ids] + GATHER_GROUP - 1) // GATHER_GROUP).astype(jnp.int32)
    return dict(dest=dest, chunk_e=chunk_e, nsub=nsub, in_blk=in_blk, out_blk=out_blk,
                sub_ids=sub_ids, src_sub=src_sub, grp_sub=grp_sub, n_act=n_act.reshape(1).astype(jnp.int32),
                n_chunks=n_chunks, n_sub_max=n_sub_max)


def _gather_kernel(src_ref, sub_ref, nact_ref, grp_ref, xp_hbm, o_ref, buf, sem, *, sb):
    i = pl.program_id(0)
    n_act = nact_ref[0]
    half = xp_hbm.shape[1]

    def issue(step, slot):
        def body(it, carry):
            for u in range(DMA_UNROLL):
                r = it * DMA_UNROLL + u
                tok = src_ref[step * sb + r]
                pltpu.make_async_copy(xp_hbm.at[pl.ds(tok, 1)], buf.at[slot, pl.ds(r, 1)],
                                      sem.at[slot]).start(priority=u % 2)
            return carry
        lax.fori_loop(0, grp_ref[step] * (GATHER_GROUP // DMA_UNROLL), body, 0)

    @pl.when(i == 0)
    def _():
        buf[...] = jnp.zeros(buf.shape, buf.dtype)
        issue(0, 0)

    @pl.when(i + 1 < n_act)
    def _():
        issue(i + 1, (i + 1) % 2)

    @pl.when(i < n_act)
    def _():
        slot = i % 2

        def wait_group(gi, carry):
            pltpu.make_async_copy(xp_hbm.at[pl.ds(0, GATHER_GROUP)], buf.at[slot, pl.ds(0, GATHER_GROUP)],
                                  sem.at[slot]).wait()
            return carry
        lax.fori_loop(0, grp_ref[i], wait_group, 0)
        rows = 32
        for c in range(sb // rows):
            u = buf[slot, c * rows:(c + 1) * rows, :]
            hi = pltpu.bitcast(u & jnp.uint32(0xFFFF0000), F32)
            lo = pltpu.bitcast(u << 16, F32)
            o_ref[c * rows:(c + 1) * rows, 0:half] = hi.astype(o_ref.dtype)
            o_ref[c * rows:(c + 1) * rows, half:2 * half] = lo.astype(o_ref.dtype)


def _gather_rows(xp, tabs):
    n, half = xp.shape
    sb = MOE_SUB
    rows = tabs['n_chunks'] * sb * MOE_SUBS_PER_CHUNK
    return pl.pallas_call(
        functools.partial(_gather_kernel, sb=sb),
        out_shape=jax.ShapeDtypeStruct((rows, 2 * half), BF16),
        grid_spec=pltpu.PrefetchScalarGridSpec(
            num_scalar_prefetch=4,
            grid=(tabs['n_sub_max'],),
            in_specs=[pl.BlockSpec(memory_space=pl.ANY)],
            out_specs=pl.BlockSpec((sb, 2 * half), lambda i, src, sub, nact, grp: (sub[i], 0)),
            scratch_shapes=[pltpu.VMEM((2, sb, half), U32), pltpu.SemaphoreType.DMA((2,))]),
        compiler_params=_params("arbitrary"),
        name="moe_gather",
    )(tabs['src_sub'], tabs['sub_ids'], tabs['n_act'], tabs['grp_sub'], xp)


def _moe_kernel(cea_ref, nsa_ref, iba_ref, ceb_ref, nsb_ref, obb_ref, x_hbm, wg_ref, wu_ref, wd_ref, o_ref,
                hid_sc, x_sc, x_sem, *, sb, n_t, tf):
    g = pl.program_id(0)
    k = pl.program_id(1)
    nsa = nsa_ref[g]
    nsb = nsb_ref[g]
    cur = g % 2
    rc = x_sc.shape[1]

    def x_copy(chunk, slot):
        row0 = pl.multiple_of(iba_ref[chunk] * rc, rc)
        return pltpu.make_async_copy(x_hbm.at[pl.ds(row0, rc)], x_sc.at[slot], x_sem.at[slot])

    @pl.when(k == 0)
    def _():
        @pl.when((g == 0) & (nsa > 0))
        def _():
            x_copy(0, 0).start()

        nxt = jnp.minimum(g + 1, pl.num_programs(0) - 1)

        @pl.when((g + 1 < pl.num_programs(0)) & (nsa_ref[nxt] > 0))
        def _():
            x_copy(nxt, 1 - cur).start()

        @pl.when(nsa > 0)
        def _():
            x_copy(g, cur).wait()

    for r in range(1, MOE_SUBS_PER_CHUNK + 1):
        rows = r * sb

        @pl.when(nsb == r)
        def _():
            o_ref[0:rows, :] = jnp.dot(hid_sc[1 - cur, 0:rows, :], wd_ref[0].astype(BF16),
                                       preferred_element_type=F32)

        @pl.when(nsa == r)
        def _():
            x = x_sc[cur, 0:rows, :]
            a = jnp.dot(x, wg_ref[0].astype(BF16), preferred_element_type=F32)
            u = jnp.dot(x, wu_ref[0].astype(BF16), preferred_element_type=F32)
            hid = (a * jax.nn.sigmoid(a) * u).astype(BF16)
            for f in range(n_t):
                @pl.when(k == f)
                def _():
                    hid_sc[cur, 0:rows, f * tf:(f + 1) * tf] = hid


def _moe_mlp(x_pad, w_gate, w_up, w_down, tabs):
    n_exp, d, ff = w_gate.shape
    sb = MOE_SUB
    rc = sb * MOE_SUBS_PER_CHUNK
    n_t = 4
    tf = ff // n_t
    td = d // n_t
    n_chunks = tabs['n_chunks']
    i32 = lambda v: jnp.asarray(v, jnp.int32).reshape(1)
    ce, ns, ib, ob = tabs['chunk_e'], tabs['nsub'], tabs['in_blk'], tabs['out_blk']
    ce_a = jnp.concatenate([ce, ce[-1:]])
    ns_a = jnp.concatenate([ns, i32(0)])
    ib_a = jnp.concatenate([ib, ib[-1:]])
    ce_b = jnp.concatenate([ce[:1], ce])
    ns_b = jnp.concatenate([i32(0), ns])
    ob_b = jnp.concatenate([i32(n_chunks + 1), ob])
    tile = lambda n, g, k: jnp.where(n[g] > 0, k, n_t - 1)
    wmap = lambda g, k, cea, nsa, iba, ceb, nsb, obb: (cea[g], 0, tile(nsa, g, k))
    return pl.pallas_call(
        functools.partial(_moe_kernel, sb=sb, n_t=n_t, tf=tf),
        out_shape=jax.ShapeDtypeStruct(((n_chunks + 2) * rc, d), F32),
        grid_spec=pltpu.PrefetchScalarGridSpec(
            num_scalar_prefetch=6,
            grid=(n_chunks + 1, n_t),
            in_specs=[pl.BlockSpec(memory_space=pl.ANY),
                      pl.BlockSpec((1, d, tf), wmap),
                      pl.BlockSpec((1, d, tf), wmap),
                      pl.BlockSpec((1, ff, td), lambda g, k, cea, nsa, iba, ceb, nsb, obb:
                                   (ceb[g], 0, tile(nsb, g, k)))],
            out_specs=pl.BlockSpec((rc, td), lambda g, k, cea, nsa, iba, ceb, nsb, obb:
                                   (obb[g], jnp.where(nsb[g] > 0, k, 0))),
            scratch_shapes=[pltpu.VMEM((2, rc, ff), BF16), pltpu.VMEM((2, rc, d), BF16),
                            pltpu.SemaphoreType.DMA((2,))]),
        compiler_params=_params("arbitrary", "arbitrary"),
        name="moe_mlp",
    )(ce_a, ns_a, ib_a, ce_b, ns_b, ob_b, x_pad, w_gate, w_up, w_down)


def _combine_kernel(dest_ref, row0_ref, y_hbm, h_hbm, gate_hbm, o_ref, ybuf, hbuf, gbuf, sem, *, tm):
    i = pl.program_id(0)
    n = pl.num_programs(0)

    def contiguous(step, slot):
        r0 = pl.multiple_of(row0_ref[step], SUBLANES)
        return (pltpu.make_async_copy(h_hbm.at[pl.ds(r0, tm)], hbuf.at[slot], sem.at[slot]),
                pltpu.make_async_copy(gate_hbm.at[pl.ds(r0, tm)], gbuf.at[slot], sem.at[slot]))

    def issue(step, slot):
        for cp in contiguous(step, slot):
            cp.start()
        base = row0_ref[step] * TOP_K

        def body(it, carry):
            for u in range(DMA_UNROLL // TOP_K):
                r = it * (DMA_UNROLL // TOP_K) + u
                for k in range(TOP_K):
                    row = dest_ref[base + r * TOP_K + k]
                    pltpu.make_async_copy(y_hbm.at[pl.ds(row, 1)], ybuf.at[slot, k, pl.ds(r, 1)],
                                          sem.at[slot]).start(priority=k % 2)
            return carry
        lax.fori_loop(0, tm * TOP_K // DMA_UNROLL, body, 0)

    @pl.when(i == 0)
    def _():
        issue(0, 0)

    @pl.when(i + 1 < n)
    def _():
        issue(i + 1, (i + 1) % 2)

    slot = i % 2
    for cp in contiguous(i, slot):
        cp.wait()
    for k in range(TOP_K):
        pltpu.make_async_copy(y_hbm.at[pl.ds(0, tm)], ybuf.at[slot, k], sem.at[slot]).wait()
    gate = gbuf[slot]
    out = hbuf[slot]
    for k in range(TOP_K):
        out = out + gate[:, k:k + 1] * ybuf[slot, k]
    o_ref[...] = out


def _combine(y_pad, h, gate, dest, row0, tm):
    d = h.shape[1]
    n_tiles = row0.shape[0]
    return pl.pallas_call(
        functools.partial(_combine_kernel, tm=tm),
        out_shape=jax.ShapeDtypeStruct((n_tiles * tm, d), F32),
        grid_spec=pltpu.PrefetchScalarGridSpec(
            num_scalar_prefetch=2,
            grid=(n_tiles,),
            in_specs=[pl.BlockSpec(memory_space=pl.ANY), pl.BlockSpec(memory_space=pl.ANY),
                      pl.BlockSpec(memory_space=pl.ANY)],
            out_specs=pl.BlockSpec((tm, d), lambda i, dest, row0: (i, 0)),
            scratch_shapes=[pltpu.VMEM((2, TOP_K, tm, d), F32), pltpu.VMEM((2, tm, d), F32),
                            pltpu.VMEM((2, tm, LANES), F32), pltpu.SemaphoreType.DMA((2,))]),
        compiler_params=_params("arbitrary"),
        name="moe_combine",
    )(dest, row0, y_pad, h, gate)


def _mixer_inputs(h, norm_g, w_in_bf, wf, bf, q_g, k_g, conv_dim, attn_w, n_heads, tm, q_scale):
    hd = attn_w // n_heads
    tn = 512
    xn = _rmsnorm(h, norm_g, _div_tile(h.shape[0], 256, 8))
    glu = _proj_glu(xn, w_in_bf, conv_dim, tm, tn // 2)
    c1 = 2 * conv_dim
    (q,) = _proj_headnorm(xn, w_in_bf, q_g, c1, attn_w, hd, q_scale, (BF16,), tm, tn)
    k32, k16 = _proj_headnorm(xn, w_in_bf, k_g, c1 + attn_w, attn_w, hd, None, (F32, BF16), tm, tn)
    v32, v16 = _proj_plain(xn, w_in_bf, c1 + 2 * attn_w, attn_w, (F32, BF16), tm, tn)
    logf = _proj_logf(xn, wf, bf, tm)[:, :n_heads]
    return glu, q, k32, k16, v32, v16, logf


def kernel(x_prompt, x_sample, cache_k, cache_v, cache_logf, cache_conv, meta_tokens, norm_mix_g, w_in, b_forget,
           q_norm_g, k_norm_g, conv_w, conv_b, conv_ln_g, conv_ln_b, w_out, norm_ffn_g, w_router_group,
           b_router_group, w_router_expert, b_router_expert, w_gate, w_up, w_down):
    depth = w_in.shape[0]
    assert depth == 1
    bp, seq, d = x_prompt.shape
    bs, tq, _ = x_sample.shape
    n_heads = cache_k.shape[3]
    hd = cache_k.shape[4]
    attn_w = n_heads * hd
    conv_dim = conv_w.shape[2]
    hist_len = conv_w.shape[1] - 1
    past = cache_k.shape[2]
    n_groups = w_router_group.shape[2]
    n_exp = w_router_expert.shape[2]
    per_group = n_exp // n_groups
    assert n_groups + n_exp <= LANES and 2 * conv_dim % 512 == 0 and attn_w % 512 == 0
    t = N_META + seq
    npr = bp * t
    ns = bs * tq
    n_all = npr + ns

    w_in_bf = w_in[0].astype(BF16)
    c4 = 2 * conv_dim + 3 * attn_w
    wf = jnp.pad(w_in[0][:, c4:], ((0, 0), (0, LANES - n_heads))).astype(BF16)
    bf = jnp.pad(b_forget[0], (0, LANES - n_heads)).reshape(1, LANES)
    w_out_bf = w_out[0].astype(BF16)
    w_r = jnp.pad(jnp.concatenate([w_router_group[0], w_router_expert[0]], axis=1),
                  ((0, 0), (0, LANES - n_groups - n_exp))).astype(BF16)
    b_r = jnp.pad(jnp.concatenate([b_router_group[0], b_router_expert[0]]),
                  (0, LANES - n_groups - n_exp)).reshape(1, LANES)

    h_p = jnp.concatenate([jnp.broadcast_to(meta_tokens[None], (bp, N_META, d)), x_prompt], axis=1).reshape(npr, d)
    h_s = x_sample.reshape(ns, d)

    tm_p = _div_tile(npr, 1400, 16)
    tm_s = _div_tile(ns, 1400, 16)
    mix = functools.partial(_mixer_inputs, norm_g=norm_mix_g[0], w_in_bf=w_in_bf, wf=wf, bf=bf, q_g=q_norm_g[0],
                            k_g=k_norm_g[0], conv_dim=conv_dim, attn_w=attn_w, n_heads=n_heads)
    glu_p, q_p, k32_p, k16_p, v32_p, v16_p, logf_p = mix(h_p, tm=tm_p, q_scale=hd ** -0.5 * LOG2E)
    glu_s, q_s, k32_s, k16_s, v32_s, v16_s, logf_s = mix(h_s, tm=tm_s, q_scale=hd ** -0.5)

    glu_p3 = glu_p.reshape(bp, t, conv_dim)
    glu_s3 = glu_s.reshape(bs, tq, conv_dim)
    hist_p = jnp.zeros((bp, hist_len + 2, conv_dim), F32)
    hist_s = jnp.pad(cache_conv[0], ((0, 0), (2, 0), (0, 0)))
    conv_args = (conv_w[0], conv_b[0], conv_ln_g[0], conv_ln_b[0])
    conv_p = _conv_module(glu_p3, hist_p, *conv_args)
    conv_s = _conv_module(glu_s3, hist_s, *conv_args)

    logf_p3 = logf_p.reshape(bp, t, n_heads)
    logf_s3 = logf_s.reshape(bs, tq, n_heads)
    cblk = 256
    pad_t = lambda a: jnp.pad(a, ((0, 0), (0, _cdiv(a.shape[1], cblk) * cblk - a.shape[1]), (0, 0)))
    cum_p = _cumsum_time(pad_t(logf_p3), cblk)[:, :t]
    lf_all = jnp.concatenate([cache_logf[0], logf_s3], axis=1)
    cum_s = _cumsum_time(pad_t(lf_all), cblk)[:, :past + tq]
    r3 = lambda a, b_, t_: a.reshape(b_, t_, attn_w)
    attn_p = _attn_prompt(r3(q_p, bp, t), r3(k16_p, bp, t), r3(v16_p, bp, t), cum_p, n_heads)
    attn_s = _attn_sample(r3(q_s, bs, tq), cache_k[0].reshape(bs, past, attn_w), cache_v[0].reshape(bs, past, attn_w),
                          r3(k16_s, bs, tq), r3(v16_s, bs, tq), cum_s, n_heads)

    tn_o = 512
    h2 = _outproj(attn_p.reshape(npr, attn_w), conv_p.reshape(npr, conv_dim), h_p,
                  attn_s.reshape(ns, attn_w), conv_s.reshape(ns, conv_dim), h_s, w_out_bf, tm_p, tn_o)

    tm_r = _div_tile(n_all, 128, 8)
    eid, gate, xp = _router(h2, n_all, norm_ffn_g[0], w_r, b_r, n_groups, per_group, tm_r)
    tabs = _dispatch_tables(eid[:, :TOP_K], n_exp)
    x_pad = _gather_rows(xp, tabs)
    y_pad = _moe_mlp(x_pad, w_gate[0], w_up[0], w_down[0], tabs)

    tm_cp = _div_tile(seq, 128, 8)
    tiles_b = seq // tm_cp
    i_p = jnp.arange(bp * tiles_b, dtype=jnp.int32)
    row0_p = (i_p // tiles_b) * t + N_META + (i_p % tiles_b) * tm_cp
    tm_cs = _div_tile(ns, 128, 8)
    row0_s = npr + jnp.arange(ns // tm_cs, dtype=jnp.int32) * tm_cs
    y_p = _combine(y_pad, h2, gate, tabs['dest'], row0_p, tm_cp).reshape(bp, seq, d)
    y_s = _combine(y_pad, h2, gate, tabs['dest'], row0_s, tm_cs).reshape(bs, tq, d)

    st = lambda a, b_, t_: a.reshape(1, b_, t_, n_heads, hd)
    return (y_p, y_s,
            st(k32_p, bp, t), st(v32_p, bp, t), logf_p3[None], glu_p3[:, t - hist_len:][None],
            st(k32_s, bs, tq), st(v32_s, bs, tq), logf_s3[None],
            jnp.concatenate([cache_conv[0], glu_s3], axis=1)[:, -hist_len:][None])
```

```python
import functools

import jax
import jax.numpy as jnp
from jax import lax
from jax.experimental import pallas as pl
from jax.experimental.pallas import tpu as pltpu

F32 = jnp.float32
BF16 = jnp.bfloat16
U32 = jnp.uint32

N_META = 16
CONV_GROUPS = 16
TOP_K = 2
EPS = 1e-6
NEG_INF = -1e30

LANES = 128
SUBLANES = 8
ATTN_BLOCK = 512
MOE_SUB = 128
MOE_SUBS_PER_CHUNK = 6
DMA_UNROLL = 8
GATHER_GROUP = 32
LOG2E = 1.4426950408889634
BIAS_PARTS = 3
VMEM_LIMIT = 56 * 1024 * 1024


def _cdiv(a, b):
    return -(-a // b)


def _gcd(a, b):
    while b:
        a, b = b, a % b
    return a


def _div_tile(n, target, mult):
    best = None
    for d in range(mult, min(n, target) + 1, mult):
        if n % d == 0:
            best = d
    assert best is not None, (n, target, mult)
    return best


def _params(*sem):
    return pltpu.CompilerParams(dimension_semantics=sem, vmem_limit_bytes=VMEM_LIMIT)


def _rmsnorm_kernel(x_ref, g_ref, o_ref):
    x = x_ref[...]
    ms = jnp.mean(x * x, axis=-1, keepdims=True)
    o_ref[...] = (x * lax.rsqrt(ms + EPS) * g_ref[...]).astype(o_ref.dtype)


def _rmsnorm(x, g, tm):
    n, d = x.shape
    return pl.pallas_call(
        _rmsnorm_kernel,
        out_shape=jax.ShapeDtypeStruct((n, d), BF16),
        grid=(n // tm,),
        in_specs=[pl.BlockSpec((tm, d), lambda i: (i, 0)), pl.BlockSpec((1, d), lambda i: (0, 0))],
        out_specs=pl.BlockSpec((tm, d), lambda i: (i, 0)),
        compiler_params=_params("parallel"),
        name="rmsnorm",
    )(x, g.reshape(1, d))


def _glu_kernel(x_ref, wa_ref, wb_ref, o_ref):
    x = x_ref[...]
    a = jnp.dot(x, wa_ref[...], preferred_element_type=F32)
    b = jnp.dot(x, wb_ref[...], preferred_element_type=F32)
    o_ref[...] = a * jax.nn.sigmoid(b)


def _proj_glu(xn, w, conv_dim, tm, tn):
    n, d = xn.shape
    nb = conv_dim // tn
    return pl.pallas_call(
        _glu_kernel,
        out_shape=jax.ShapeDtypeStruct((n, conv_dim), F32),
        grid=(n // tm, nb),
        in_specs=[pl.BlockSpec((tm, d), lambda i, j: (i, 0)),
                  pl.BlockSpec((d, tn), lambda i, j: (0, j)),
                  pl.BlockSpec((d, tn), lambda i, j: (0, nb + j))],
        out_specs=pl.BlockSpec((tm, tn), lambda i, j: (i, j)),
        compiler_params=_params("parallel", "arbitrary"),
        name="proj_glu",
    )(xn, w, w)


def _headnorm_kernel(x_ref, w_ref, g_ref, *o_refs, scale, hd):
    z = jnp.dot(x_ref[...], w_ref[...], preferred_element_type=F32)
    g = g_ref[...]
    for h in range(z.shape[1] // hd):
        zh = z[:, h * hd:(h + 1) * hd]
        y = zh * lax.rsqrt(jnp.mean(zh * zh, axis=-1, keepdims=True) + EPS) * g
        if scale is not None:
            y = y * scale
        for o_ref in o_refs:
            o_ref[:, h * hd:(h + 1) * hd] = y.astype(o_ref.dtype)


def _proj_headnorm(xn, w, g, col0, width, hd, scale, out_dtypes, tm, tn):
    n, d = xn.shape
    nb = width // tn
    b0 = col0 // tn
    outs = pl.pallas_call(
        functools.partial(_headnorm_kernel, scale=scale, hd=hd),
        out_shape=[jax.ShapeDtypeStruct((n, width), dt) for dt in out_dtypes],
        grid=(n // tm, nb),
        in_specs=[pl.BlockSpec((tm, d), lambda i, j: (i, 0)),
                  pl.BlockSpec((d, tn), lambda i, j: (0, b0 + j)),
                  pl.BlockSpec((1, hd), lambda i, j: (0, 0))],
        out_specs=[pl.BlockSpec((tm, tn), lambda i, j: (i, j)) for _ in out_dtypes],
        compiler_params=_params("parallel", "arbitrary"),
        name="proj_headnorm",
    )(xn, w, g.reshape(1, hd))
    return outs


def _plain_kernel(x_ref, w_ref, *o_refs):
    z = jnp.dot(x_ref[...], w_ref[...], preferred_element_type=F32)
    for o_ref in o_refs:
        o_ref[...] = z.astype(o_ref.dtype)


def _proj_plain(xn, w, col0, width, out_dtypes, tm, tn):
    n, d = xn.shape
    nb = width // tn
    b0 = col0 // tn
    return pl.pallas_call(
        _plain_kernel,
        out_shape=[jax.ShapeDtypeStruct((n, width), dt) for dt in out_dtypes],
        grid=(n // tm, nb),
        in_specs=[pl.BlockSpec((tm, d), lambda i, j: (i, 0)),
                  pl.BlockSpec((d, tn), lambda i, j: (0, b0 + j))],
        out_specs=[pl.BlockSpec((tm, tn), lambda i, j: (i, j)) for _ in out_dtypes],
        compiler_params=_params("parallel", "arbitrary"),
        name="proj_plain",
    )(xn, w)


def _logf_kernel(x_ref, w_ref, b_ref, o_ref):
    z = jnp.dot(x_ref[...], w_ref[...], preferred_element_type=F32) + b_ref[...]
    o_ref[...] = jnp.minimum(z, 0.0) - jnp.log1p(jnp.exp(-jnp.abs(z)))


def _proj_logf(xn, wf, bf, tm):
    n, d = xn.shape
    return pl.pallas_call(
        _logf_kernel,
        out_shape=jax.ShapeDtypeStruct((n, LANES), F32),
        grid=(n // tm,),
        in_specs=[pl.BlockSpec((tm, d), lambda i: (i, 0)),
                  pl.BlockSpec((d, LANES), lambda i: (0, 0)),
                  pl.BlockSpec((1, LANES), lambda i: (0, 0))],
        out_specs=pl.BlockSpec((tm, LANES), lambda i: (i, 0)),
        compiler_params=_params("parallel"),
        name="proj_logf",
    )(xn, wf, bf)


def _conv_kernel(glu_ref, hist_ref, cw_ref, cb_ref, lg_ref, lb_ref, o_ref, full_sc, *, t, tt, width):
    hp = hist_ref.shape[1]
    lanes = glu_ref.shape[2]
    full_sc[0:hp, :] = hist_ref[0]
    full_sc[hp:hp + t, :] = glu_ref[0]
    full_sc[hp + t:hp + t + SUBLANES, :] = jnp.zeros((SUBLANES, lanes), F32)

    def chunk(ci, carry):
        t0 = pl.multiple_of(ci * tt, 16)
        for lt in range(lanes // LANES):
            ls = slice(lt * LANES, (lt + 1) * LANES)
            win = full_sc[pl.ds(t0, tt + hp + SUBLANES), ls]
            y = None
            for r in range(SUBLANES):
                part = None
                for a in range(_cdiv(width + 2, SUBLANES)):
                    j = SUBLANES * a + r
                    if not 2 <= j < width + 2:
                        continue
                    term = win[SUBLANES * a:SUBLANES * a + tt + SUBLANES, :] * cw_ref[j - 2:j - 1, ls]
                    part = term if part is None else part + term
                if part is None:
                    continue
                shifted = part[r:r + tt, :]
                y = shifted if y is None else y + shifted
            y = y + cb_ref[:, ls]
            mu = jnp.mean(y, axis=-1, keepdims=True)
            dlt = y - mu
            var = jnp.mean(dlt * dlt, axis=-1, keepdims=True)
            yn = dlt * lax.rsqrt(var + EPS) * lg_ref[:, ls] + lb_ref[:, ls]
            o_ref[0, pl.ds(t0, tt), ls] = (yn * jax.nn.sigmoid(yn)).astype(o_ref.dtype)
        return carry

    lax.fori_loop(0, t // tt, chunk, 0)


def _conv_module(glu, hist, conv_w, conv_b, ln_g, ln_b):
    b, t, c = glu.shape
    width = conv_w.shape[0]
    hp = hist.shape[1]
    assert hp == 32 and width + 1 == hp and c // CONV_GROUPS == LANES
    tt = _div_tile(t, 64, 16)
    tc = 8 * LANES
    cw = jnp.concatenate([conv_w, jnp.zeros((hp - width, c), F32)], axis=0)
    vec = lambda v: v.reshape(1, c)
    return pl.pallas_call(
        functools.partial(_conv_kernel, t=t, tt=tt, width=width),
        out_shape=jax.ShapeDtypeStruct((b, t, c), BF16),
        grid=(b, c // tc),
        in_specs=[pl.BlockSpec((1, t, tc), lambda i, j: (i, 0, j)),
                  pl.BlockSpec((1, hp, tc), lambda i, j: (i, 0, j)),
                  pl.BlockSpec((hp, tc), lambda i, j: (0, j)),
                  pl.BlockSpec((1, tc), lambda i, j: (0, j)),
                  pl.BlockSpec((1, tc), lambda i, j: (0, j)),
                  pl.BlockSpec((1, tc), lambda i, j: (0, j))],
        out_specs=pl.BlockSpec((1, t, tc), lambda i, j: (i, 0, j)),
        scratch_shapes=[pltpu.VMEM((hp + t + SUBLANES, tc), F32)],
        compiler_params=_params("parallel", "parallel"),
        name="conv_ln_swish",
    )(glu, hist, cw, vec(conv_b), vec(ln_g), vec(ln_b))


def _cumsum_kernel(x_ref, o_ref, *, nblk, blk):
    r = lax.broadcasted_iota(jnp.int32, (blk, blk), 0)
    c = lax.broadcasted_iota(jnp.int32, (blk, blk), 1)
    tri = (c <= r).astype(F32)
    carry = jnp.zeros((1, x_ref.shape[2]), F32)
    for i in range(nblk):
        x = x_ref[0, i * blk:(i + 1) * blk, :]
        o_ref[0, i * blk:(i + 1) * blk, :] = jnp.dot(
            tri, x, precision=lax.Precision.HIGHEST, preferred_element_type=F32) + carry
        carry = carry + jnp.sum(x, axis=0, keepdims=True)


def _cumsum_time(x, blk):
    b, t, h = x.shape
    return pl.pallas_call(
        functools.partial(_cumsum_kernel, nblk=t // blk, blk=blk),
        out_shape=jax.ShapeDtypeStruct((b, t, h), F32),
        grid=(b,),
        in_specs=[pl.BlockSpec((1, t, h), lambda i: (i, 0, 0))],
        out_specs=pl.BlockSpec((1, t, h), lambda i: (i, 0, 0)),
        compiler_params=_params("parallel"),
        name="logf_cumsum",
    )(x)


def _attn_update(q, k, v, cq, ck, mask, m, l, acc):
    s = lax.dot_general(q, k, (((1,), (1,)), ((), ())), preferred_element_type=F32)
    s = s - ck
    if mask is not None:
        s = jnp.where(mask, s, NEG_INF)
    m_new = jnp.maximum(m, jnp.max(s, axis=-1, keepdims=True) + cq)
    alpha = jnp.exp(m - m_new)
    p = jnp.exp(s + (cq - m_new))
    l_new = alpha * l + jnp.sum(p, axis=-1, keepdims=True)
    acc_new = alpha * acc + jnp.dot(p.astype(BF16), v, preferred_element_type=F32)
    return m_new, l_new, acc_new


def _softmax_init(tq, d):
    return (jnp.full((tq, 1), NEG_INF, F32), jnp.zeros((tq, 1), F32), jnp.zeros((tq, d), F32))


def _head_column(c, h):
    sel = lax.broadcasted_iota(jnp.int32, c.shape, 1) == h
    return jnp.sum(jnp.where(sel, c, 0.0), axis=-1, keepdims=True)


def _causal_mask(n):
    r = lax.broadcasted_iota(jnp.int32, (n, n), 0)
    c = lax.broadcasted_iota(jnp.int32, (n, n), 1)
    return c <= r


def _attn_prompt_kernel(q_ref, k_ref, v_ref, cum_ref, o_ref, qa_sc, ka_sc, s_sc, d_sc, p_sc, pd_sc, l_sc,
                        *, t, bq, rs):
    h = pl.program_id(1)
    d = q_ref.shape[2]
    nfull = t // bq
    tail = t - nfull * bq
    nt = (((1,), (1,)), ((), ()))

    lane = lax.broadcasted_iota(jnp.int32, (t, d), 1)
    rem = _head_column(cum_ref[0], h) * LOG2E
    aug = jnp.zeros((t, d), F32)
    for j in range(BIAS_PARTS):
        part = rem.astype(BF16).astype(F32)
        aug = jnp.where(lane == j, part, aug)
        rem = rem - part
    ka_sc[:, 0:d] = k_ref[0]
    ka_sc[:, d:2 * d] = aug.astype(BF16)
    qa_sc[:, 0:d] = q_ref[0]
    qa_sc[:, d:2 * d] = jnp.where(lane < BIAS_PARTS, -1.0, 0.0).astype(BF16)

    blocks = [(i * bq, bq) for i in range(nfull)] + ([(nfull * bq, tail)] if tail else [])
    for q0, nq in blocks:
        q = qa_sc[q0:q0 + nq, :]
        if q0:
            s_sc[0:nq, 0:q0] = lax.dot_general(q, ka_sc[0:q0, :], nt, preferred_element_type=F32)
        d_sc[0:nq, 0:nq] = lax.dot_general(q, ka_sc[q0:q0 + nq, :], nt, preferred_element_type=F32)
        rows = min(rs, nq)

        def softmax_rows(ri, carry, q0=q0, nq=nq, rows=rows):
            r0 = ri * rows
            cq = _head_column(cum_ref[0, pl.ds(q0 + r0, rows), :], h) * LOG2E
            rr = lax.broadcasted_iota(jnp.int32, (rows, nq), 0) + r0
            cc = lax.broadcasted_iota(jnp.int32, (rows, nq), 1)
            sd = jnp.where(cc <= rr, d_sc[pl.ds(r0, rows), 0:nq], NEG_INF)
            mx = jnp.max(sd, axis=-1, keepdims=True)
            if q0:
                sp = s_sc[pl.ds(r0, rows), 0:q0]
                mx = jnp.maximum(mx, jnp.max(sp, axis=-1, keepdims=True))
            shift = cq - (mx + cq)
            pd = jnp.exp2(sd + shift)
            l = jnp.sum(pd, axis=-1, keepdims=True)
            pd_sc[pl.ds(r0, rows), 0:nq] = pd.astype(BF16)
            if q0:
                pp = jnp.exp2(sp + shift)
                l = l + jnp.sum(pp, axis=-1, keepdims=True)
                p_sc[pl.ds(r0, rows), 0:q0] = pp.astype(BF16)
            l_sc[pl.ds(r0, rows), :] = l
            return carry

        for ri in range(nq // rows):
            softmax_rows(ri, 0)
        o = jnp.dot(pd_sc[0:nq, 0:nq], v_ref[0, q0:q0 + nq, :], preferred_element_type=F32)
        if q0:
            o = o + jnp.dot(p_sc[0:nq, 0:q0], v_ref[0, 0:q0, :], preferred_element_type=F32)
        o_ref[0, q0:q0 + nq, :] = (o / l_sc[0:nq, :]).astype(o_ref.dtype)


def _blocked_rows(cum, blk):
    b, t, h = cum.shape
    nblk = _cdiv(t, blk)
    c = jnp.pad(cum, ((0, 0), (0, nblk * blk - t), (0, 0)))
    return c.transpose(0, 2, 1).reshape(b, h, nblk, 1, blk)


def _attn_prompt(q, k, v, cum, n_heads):
    b, t, w = q.shape
    d = w // n_heads
    bq = ATTN_BLOCK
    nfull = t // bq
    assert nfull >= 1 and (t % bq) % 16 == 0
    blk = lambda: pl.BlockSpec((1, t, d), lambda i, j: (i, 0, j))
    return pl.pallas_call(
        functools.partial(_attn_prompt_kernel, t=t, bq=bq, rs=128),
        out_shape=jax.ShapeDtypeStruct((b, t, w), BF16),
        grid=(b, n_heads),
        in_specs=[blk(), blk(), blk(),
                  pl.BlockSpec((1, t, n_heads), lambda i, j: (i, 0, 0))],
        out_specs=blk(),
        scratch_shapes=[pltpu.VMEM((t, 2 * d), BF16), pltpu.VMEM((t, 2 * d), BF16),
                        pltpu.VMEM((bq, nfull * bq), F32), pltpu.VMEM((bq, bq), F32),
                        pltpu.VMEM((bq, nfull * bq), BF16), pltpu.VMEM((bq, bq), BF16),
                        pltpu.VMEM((bq, 1), F32)],
        compiler_params=_params("parallel", "parallel"),
        name="fox_attn_prompt",
    )(q, k, v, cum)


def _attn_sample_kernel(q_ref, ck_ref, cv_ref, k_ref, v_ref, cum_ref, cumt_ref, o_ref, *, past, bk, hpb, d):
    tq = q_ref.shape[1]
    for hh in range(hpb):
        cols = slice(hh * d, (hh + 1) * d)
        q = q_ref[0, :, cols]
        cq = _head_column(cum_ref[0], pl.program_id(1) * hpb + hh)
        carry = _softmax_init(tq, d)
        for kj in range(past // bk):
            kb = ck_ref[0, kj * bk:(kj + 1) * bk, cols].astype(BF16)
            vb = cv_ref[0, kj * bk:(kj + 1) * bk, cols].astype(BF16)
            carry = _attn_update(q, kb, vb, cq, cumt_ref[0, hh, kj], None, *carry)
        carry = _attn_update(q, k_ref[0, :, cols], v_ref[0, :, cols], cq, cumt_ref[0, hh, past // bk][:, :tq],
                             _causal_mask(tq), *carry)
        _, l, acc = carry
        o_ref[0, :, cols] = (acc / l).astype(o_ref.dtype)


def _attn_sample(q, cache_k, cache_v, k, v, cum, n_heads):
    b, tq, w = q.shape
    d = w // n_heads
    past = cache_k.shape[1]
    bk = _div_tile(past, ATTN_BLOCK, LANES)
    assert tq <= bk
    cumt = _blocked_rows(cum, bk)
    nblk = cumt.shape[2]
    cum_q = cum[:, past:]
    hpb = 4 if n_heads % 4 == 0 else 1
    new = lambda: pl.BlockSpec((1, tq, hpb * d), lambda i, j: (i, 0, j))
    old = lambda: pl.BlockSpec((1, past, hpb * d), lambda i, j: (i, 0, j))
    return pl.pallas_call(
        functools.partial(_attn_sample_kernel, past=past, bk=bk, hpb=hpb, d=d),
        out_shape=jax.ShapeDtypeStruct((b, tq, w), BF16),
        grid=(b, n_heads // hpb),
        in_specs=[new(), old(), old(), new(), new(),
                  pl.BlockSpec((1, tq, n_heads), lambda i, j: (i, 0, 0)),
                  pl.BlockSpec((1, hpb, nblk, 1, bk), lambda i, j: (i, j, 0, 0, 0))],
        out_specs=new(),
        compiler_params=_params("parallel", "parallel"),
        name="fox_attn_sample",
    )(q, cache_k, cache_v, k, v, cum_q, cumt)


def _outproj_kernel(ap_ref, cp_ref, hp_ref, as_ref, cs_ref, hs_ref, wa_ref, wc_ref, o_ref, *, n_p):
    i = pl.program_id(0)

    def project(a_ref, c_ref, h_ref):
        return (h_ref[...]
                + jnp.dot(a_ref[...], wa_ref[...], preferred_element_type=F32)
                + jnp.dot(c_ref[...], wc_ref[...], preferred_element_type=F32))

    @pl.when(i < n_p)
    def _():
        o_ref[...] = project(ap_ref, cp_ref, hp_ref)

    @pl.when(i == n_p)
    def _():
        o_ref[0:hs_ref.shape[0], :] = project(as_ref, cs_ref, hs_ref)


def _outproj(attn_p, conv_p, h_p, attn_s, conv_s, h_s, w_out, tm, tn):
    npr, wa = attn_p.shape
    ns = attn_s.shape[0]
    wc = conv_p.shape[1]
    d = w_out.shape[1]
    assert wa == wc and npr % tm == 0 and ns <= tm
    n_p = npr // tm
    last = lambda i: jnp.minimum(i, n_p - 1)
    return pl.pallas_call(
        functools.partial(_outproj_kernel, n_p=n_p),
        out_shape=jax.ShapeDtypeStruct(((n_p + 1) * tm, d), F32),
        grid=(n_p + 1, d // tn),
        in_specs=[pl.BlockSpec((tm, wa), lambda i, j: (last(i), 0)),
                  pl.BlockSpec((tm, wc), lambda i, j: (last(i), 0)),
                  pl.BlockSpec((tm, tn), lambda i, j: (last(i), j)),
                  pl.BlockSpec((ns, wa), lambda i, j: (0, 0)),
                  pl.BlockSpec((ns, wc), lambda i, j: (0, 0)),
                  pl.BlockSpec((ns, tn), lambda i, j: (0, j)),
                  pl.BlockSpec((wa, tn), lambda i, j: (0, j)),
                  pl.BlockSpec((wc, tn), lambda i, j: (1, j))],
        out_specs=pl.BlockSpec((tm, tn), lambda i, j: (i, j)),
        compiler_params=_params("parallel", "arbitrary"),
        name="outproj_residual",
    )(attn_p, conv_p, h_p, attn_s, conv_s, h_s, w_out, w_out)


def _router_kernel(h_ref, g_ref, w_ref, b_ref, eid_ref, gate_ref, xp_ref, *, n_groups, per_group):
    x = h_ref[...]
    xn = (x * lax.rsqrt(jnp.mean(x * x, axis=-1, keepdims=True) + EPS) * g_ref[...]).astype(BF16)
    half = x.shape[1] // 2
    bits = pltpu.bitcast(xn.astype(F32), U32)
    xp_ref[...] = bits[:, :half] | (bits[:, half:] >> 16)

    lg = jnp.dot(xn, w_ref[...], preferred_element_type=F32) + b_ref[...]
    lane = lax.broadcasted_iota(jnp.int32, lg.shape, 1)
    ninf = -jnp.inf

    def top1(vals):
        mx = jnp.max(vals, axis=-1, keepdims=True)
        idx = jnp.min(jnp.where(vals == mx, lane, LANES), axis=-1, keepdims=True)
        return mx, idx

    gl = jnp.where(lane < n_groups, lg, ninf)
    gmax, gidx = top1(gl)
    p_g = 1.0 / jnp.sum(jnp.exp(gl - gmax), axis=-1, keepdims=True)
    lo = n_groups + per_group * gidx
    el = jnp.where((lane >= lo) & (lane < lo + per_group), lg, ninf)
    m1, i1 = top1(el)
    m2, i2 = top1(jnp.where(lane == i1, ninf, el))
    e21 = jnp.exp(m2 - m1)
    g1 = p_g / (1.0 + e21)
    g2 = g1 * e21
    eid_ref[...] = jnp.where(lane == 0, i1 - n_groups, jnp.where(lane == 1, i2 - n_groups, 0))
    gate_ref[...] = jnp.where(lane == 0, g1, jnp.where(lane == 1, g2, 0.0))


def _router(h, n, g, w_r, b_r, n_groups, per_group, tm):
    d = h.shape[1]
    row = lambda w: pl.BlockSpec((tm, w), lambda i: (i, 0))
    return pl.pallas_call(
        functools.partial(_router_kernel, n_groups=n_groups, per_group=per_group),
        out_shape=[jax.ShapeDtypeStruct((n, LANES), jnp.int32), jax.ShapeDtypeStruct((n, LANES), F32),
                   jax.ShapeDtypeStruct((n, d // 2), U32)],
        grid=(n // tm,),
        in_specs=[row(d),
                  pl.BlockSpec((1, d), lambda i: (0, 0)),
                  pl.BlockSpec((d, LANES), lambda i: (0, 0)),
                  pl.BlockSpec((1, LANES), lambda i: (0, 0))],
        out_specs=[row(LANES), row(LANES), row(d // 2)],
        compiler_params=_params("parallel"),
        name="router",
    )(h, g.reshape(1, d), w_r, b_r)


def _dispatch_tables(e_id, n_exp):
    n, k = e_id.shape
    m = n * k
    sb = MOE_SUB
    rc = sb * MOE_SUBS_PER_CHUNK
    n_chunks = _cdiv(m, rc) + n_exp
    n_sub_max = _cdiv(m, sb) + n_exp
    e_flat = e_id.reshape(m)
    onehot = (e_flat[:, None] == jnp.arange(n_exp, dtype=jnp.int32)[None, :]).astype(jnp.int32)
    csum = jnp.cumsum(onehot, axis=0)
    counts = csum[-1]
    rank = jnp.take_along_axis(csum, e_flat[:, None], axis=1)[:, 0] - 1
    cpe = (counts + rc - 1) // rc
    chunk_end = jnp.cumsum(cpe)
    chunk_start = chunk_end - cpe
    n_used = chunk_end[-1]
    dest = (chunk_start[e_flat] * rc + rank).astype(jnp.int32)
    c_ar = jnp.arange(n_chunks, dtype=jnp.int32)
    used = c_ar < n_used
    c_cl = jnp.minimum(c_ar, n_used - 1)
    chunk_e = jnp.minimum(jnp.searchsorted(chunk_end, c_cl, side='right'), n_exp - 1).astype(jnp.int32)
    rows = jnp.clip(counts[chunk_e] - (c_cl - chunk_start[chunk_e]) * rc, 0, rc)
    nsub = jnp.where(used, (rows + sb - 1) // sb, 0).astype(jnp.int32)
    in_blk = c_cl.astype(jnp.int32)
    out_blk = jnp.where(used, c_ar, n_chunks).astype(jnp.int32)
    tok = (jnp.arange(m, dtype=jnp.int32) // k)
    src = jnp.zeros((n_chunks * rc,), jnp.int32).at[dest].set(tok)
    sub_active = (jnp.arange(MOE_SUBS_PER_CHUNK, dtype=jnp.int32)[None, :] < nsub[:, None]).reshape(-1)
    n_act = jnp.sum(sub_active.astype(jnp.int32))
    sub_ids = jnp.nonzero(sub_active, size=n_sub_max, fill_value=0)[0].astype(jnp.int32)
    sub_ids = jnp.where(jnp.arange(n_sub_max) < n_act, sub_ids, sub_ids[n_act - 1])
    src_sub = src.reshape(n_chunks * MOE_SUBS_PER_CHUNK, sb)[sub_ids].reshape(-1)
    rows_sub = jnp.clip(rows[:, None] - jnp.arange(MOE_SUBS_PER_CHUNK, dtype=jnp.int32)[None, :] * sb, 0, sb)
    grp_sub = ((rows_sub.reshape(-1)[sub_ids] + GATHER_GROUP - 1) // GATHER_GROUP).astype(jnp.int32)
    return dict(dest=dest, chunk_e=chunk_e, nsub=nsub, in_blk=in_blk, out_blk=out_blk,
                sub_ids=sub_ids, src_sub=src_sub, grp_sub=grp_sub, n_act=n_act.reshape(1).astype(jnp.int32),
                n_chunks=n_chunks, n_sub_max=n_sub_max)


def _gather_kernel(src_ref, sub_ref, nact_ref, grp_ref, xp_hbm, o_ref, buf, sem, *, sb):
    i = pl.program_id(0)
    n_act = nact_ref[0]
    half = xp_hbm.shape[1]

    def issue(step, slot):
        def body(it, carry):
            for u in range(DMA_UNROLL):
                r = it * DMA_UNROLL + u
                tok = src_ref[step * sb + r]
                pltpu.make_async_copy(xp_hbm.at[pl.ds(tok, 1)], buf.at[slot, pl.ds(r, 1)], sem.at[slot]).start()
            return carry
        lax.fori_loop(0, grp_ref[step] * (GATHER_GROUP // DMA_UNROLL), body, 0)

    @pl.when(i == 0)
    def _():
        buf[...] = jnp.zeros(buf.shape, buf.dtype)
        issue(0, 0)

    @pl.when(i + 1 < n_act)
    def _():
        issue(i + 1, (i + 1) % 2)

    @pl.when(i < n_act)
    def _():
        slot = i % 2

        def wait_group(gi, carry):
            pltpu.make_async_copy(xp_hbm.at[pl.ds(0, GATHER_GROUP)], buf.at[slot, pl.ds(0, GATHER_GROUP)],
                                  sem.at[slot]).wait()
            return carry
        lax.fori_loop(0, grp_ref[i], wait_group, 0)
        rows = 32
        for c in range(sb // rows):
            u = buf[slot, c * rows:(c + 1) * rows, :]
            hi = pltpu.bitcast(u & jnp.uint32(0xFFFF0000), F32)
            lo = pltpu.bitcast(u << 16, F32)
            o_ref[c * rows:(c + 1) * rows, 0:half] = hi.astype(o_ref.dtype)
            o_ref[c * rows:(c + 1) * rows, half:2 * half] = lo.astype(o_ref.dtype)


def _gather_rows(xp, tabs):
    n, half = xp.shape
    sb = MOE_SUB
    rows = tabs['n_chunks'] * sb * MOE_SUBS_PER_CHUNK
    return pl.pallas_call(
        functools.partial(_gather_kernel, sb=sb),
        out_shape=jax.ShapeDtypeStruct((rows, 2 * half), BF16),
        grid_spec=pltpu.PrefetchScalarGridSpec(
            num_scalar_prefetch=4,
            grid=(tabs['n_sub_max'],),
            in_specs=[pl.BlockSpec(memory_space=pl.ANY)],
            out_specs=pl.BlockSpec((sb, 2 * half), lambda i, src, sub, nact, grp: (sub[i], 0)),
            scratch_shapes=[pltpu.VMEM((2, sb, half), U32), pltpu.SemaphoreType.DMA((2,))]),
        compiler_params=_params("arbitrary"),
        name="moe_gather",
    )(tabs['src_sub'], tabs['sub_ids'], tabs['n_act'], tabs['grp_sub'], xp)


def _moe_kernel(cea_ref, nsa_ref, iba_ref, ceb_ref, nsb_ref, obb_ref, x_hbm, wg_ref, wu_ref, wd_ref, o_ref,
                hid_sc, x_sc, x_sem, *, sb, n_t, tf):
    g = pl.program_id(0)
    k = pl.program_id(1)
    nsa = nsa_ref[g]
    nsb = nsb_ref[g]
    cur = g % 2
    rc = x_sc.shape[1]

    def x_copy(chunk, slot):
        row0 = pl.multiple_of(iba_ref[chunk] * rc, rc)
        return pltpu.make_async_copy(x_hbm.at[pl.ds(row0, rc)], x_sc.at[slot], x_sem.at[slot])

    @pl.when(k == 0)
    def _():
        @pl.when((g == 0) & (nsa > 0))
        def _():
            x_copy(0, 0).start()

        nxt = jnp.minimum(g + 1, pl.num_programs(0) - 1)

        @pl.when((g + 1 < pl.num_programs(0)) & (nsa_ref[nxt] > 0))
        def _():
            x_copy(nxt, 1 - cur).start()

        @pl.when(nsa > 0)
        def _():
            x_copy(g, cur).wait()

    for r in range(1, MOE_SUBS_PER_CHUNK + 1):
        rows = r * sb

        @pl.when(nsb == r)
        def _():
            o_ref[0:rows, :] = jnp.dot(hid_sc[1 - cur, 0:rows, :], wd_ref[0].astype(BF16),
                                       preferred_element_type=F32)

        @pl.when(nsa == r)
        def _():
            x = x_sc[cur, 0:rows, :]
            a = jnp.dot(x, wg_ref[0].astype(BF16), preferred_element_type=F32)
            u = jnp.dot(x, wu_ref[0].astype(BF16), preferred_element_type=F32)
            hid = (a * jax.nn.sigmoid(a) * u).astype(BF16)
            for f in range(n_t):
                @pl.when(k == f)
                def _():
                    hid_sc[cur, 0:rows, f * tf:(f + 1) * tf] = hid


def _moe_mlp(x_pad, w_gate, w_up, w_down, tabs):
    n_exp, d, ff = w_gate.shape
    sb = MOE_SUB
    rc = sb * MOE_SUBS_PER_CHUNK
    n_t = 4
    tf = ff // n_t
    td = d // n_t
    n_chunks = tabs['n_chunks']
    i32 = lambda v: jnp.asarray(v, jnp.int32).reshape(1)
    ce, ns, ib, ob = tabs['chunk_e'], tabs['nsub'], tabs['in_blk'], tabs['out_blk']
    ce_a = jnp.concatenate([ce, ce[-1:]])
    ns_a = jnp.concatenate([ns, i32(0)])
    ib_a = jnp.concatenate([ib, ib[-1:]])
    ce_b = jnp.concatenate([ce[:1], ce])
    ns_b = jnp.concatenate([i32(0), ns])
    ob_b = jnp.concatenate([i32(n_chunks + 1), ob])
    tile = lambda n, g, k: jnp.where(n[g] > 0, k, n_t - 1)
    wmap = lambda g, k, cea, nsa, iba, ceb, nsb, obb: (cea[g], 0, tile(nsa, g, k))
    return pl.pallas_call(
        functools.partial(_moe_kernel, sb=sb, n_t=n_t, tf=tf),
        out_shape=jax.ShapeDtypeStruct(((n_chunks + 2) * rc, d), F32),
        grid_spec=pltpu.PrefetchScalarGridSpec(
            num_scalar_prefetch=6,
            grid=(n_chunks + 1, n_t),
            in_specs=[pl.BlockSpec(memory_space=pl.ANY),
                      pl.BlockSpec((1, d, tf), wmap),
                      pl.BlockSpec((1, d, tf), wmap),
                      pl.BlockSpec((1, ff, td), lambda g, k, cea, nsa, iba, ceb, nsb, obb:
                                   (ceb[g], 0, tile(nsb, g, k)))],
            out_specs=pl.BlockSpec((rc, td), lambda g, k, cea, nsa, iba, ceb, nsb, obb:
                                   (obb[g], jnp.where(nsb[g] > 0, k, 0))),
            scratch_shapes=[pltpu.VMEM((2, rc, ff), BF16), pltpu.VMEM((2, rc, d), BF16),
                            pltpu.SemaphoreType.DMA((2,))]),
        compiler_params=_params("arbitrary", "arbitrary"),
        name="moe_mlp",
    )(ce_a, ns_a, ib_a, ce_b, ns_b, ob_b, x_pad, w_gate, w_up, w_down)


def _combine_kernel(dest_ref, row0_ref, y_hbm, h_hbm, gate_hbm, o_ref, ybuf, hbuf, gbuf, sem, *, tm):
    i = pl.program_id(0)
    n = pl.num_programs(0)

    def contiguous(step, slot):
        r0 = pl.multiple_of(row0_ref[step], SUBLANES)
        return (pltpu.make_async_copy(h_hbm.at[pl.ds(r0, tm)], hbuf.at[slot], sem.at[slot]),
                pltpu.make_async_copy(gate_hbm.at[pl.ds(r0, tm)], gbuf.at[slot], sem.at[slot]))

    def issue(step, slot):
        for cp in contiguous(step, slot):
            cp.start()
        base = row0_ref[step] * TOP_K

        def body(it, carry):
            for u in range(DMA_UNROLL // TOP_K):
                r = it * (DMA_UNROLL // TOP_K) + u
                for k in range(TOP_K):
                    row = dest_ref[base + r * TOP_K + k]
                    pltpu.make_async_copy(y_hbm.at[pl.ds(row, 1)], ybuf.at[slot, k, pl.ds(r, 1)],
                                          sem.at[slot]).start()
            return carry
        lax.fori_loop(0, tm * TOP_K // DMA_UNROLL, body, 0)

    @pl.when(i == 0)
    def _():
        issue(0, 0)

    @pl.when(i + 1 < n)
    def _():
        issue(i + 1, (i + 1) % 2)

    slot = i % 2
    for cp in contiguous(i, slot):
        cp.wait()
    for k in range(TOP_K):
        pltpu.make_async_copy(y_hbm.at[pl.ds(0, tm)], ybuf.at[slot, k], sem.at[slot]).wait()
    gate = gbuf[slot]
    out = hbuf[slot]
    for k in range(TOP_K):
        out = out + gate[:, k:k + 1] * ybuf[slot, k]
    o_ref[...] = out


def _combine(y_pad, h, gate, dest, row0, tm):
    d = h.shape[1]
    n_tiles = row0.shape[0]
    return pl.pallas_call(
        functools.partial(_combine_kernel, tm=tm),
        out_shape=jax.ShapeDtypeStruct((n_tiles * tm, d), F32),
        grid_spec=pltpu.PrefetchScalarGridSpec(
            num_scalar_prefetch=2,
            grid=(n_tiles,),
            in_specs=[pl.BlockSpec(memory_space=pl.ANY), pl.BlockSpec(memory_space=pl.ANY),
                      pl.BlockSpec(memory_space=pl.ANY)],
            out_specs=pl.BlockSpec((tm, d), lambda i, dest, row0: (i, 0)),
            scratch_shapes=[pltpu.VMEM((2, TOP_K, tm, d), F32), pltpu.VMEM((2, tm, d), F32),
                            pltpu.VMEM((2, tm, LANES), F32), pltpu.SemaphoreType.DMA((2,))]),
        compiler_params=_params("arbitrary"),
        name="moe_combine",
    )(dest, row0, y_pad, h, gate)


def _mixer_inputs(h, norm_g, w_in_bf, wf, bf, q_g, k_g, conv_dim, attn_w, n_heads, tm, q_scale):
    hd = attn_w // n_heads
    tn = 512
    xn = _rmsnorm(h, norm_g, _div_tile(h.shape[0], 256, 8))
    glu = _proj_glu(xn, w_in_bf, conv_dim, tm, tn // 2)
    c1 = 2 * conv_dim
    (q,) = _proj_headnorm(xn, w_in_bf, q_g, c1, attn_w, hd, q_scale, (BF16,), tm, tn)
    k32, k16 = _proj_headnorm(xn, w_in_bf, k_g, c1 + attn_w, attn_w, hd, None, (F32, BF16), tm, tn)
    v32, v16 = _proj_plain(xn, w_in_bf, c1 + 2 * attn_w, attn_w, (F32, BF16), tm, tn)
    logf = _proj_logf(xn, wf, bf, tm)[:, :n_heads]
    return glu, q, k32, k16, v32, v16, logf


def kernel(x_prompt, x_sample, cache_k, cache_v, cache_logf, cache_conv, meta_tokens, norm_mix_g, w_in, b_forget,
           q_norm_g, k_norm_g, conv_w, conv_b, conv_ln_g, conv_ln_b, w_out, norm_ffn_g, w_router_group,
           b_router_group, w_router_expert, b_router_expert, w_gate, w_up, w_down):
    depth = w_in.shape[0]
    assert depth == 1
    bp, seq, d = x_prompt.shape
    bs, tq, _ = x_sample.shape
    n_heads = cache_k.shape[3]
    hd = cache_k.shape[4]
    attn_w = n_heads * hd
    conv_dim = conv_w.shape[2]
    hist_len = conv_w.shape[1] - 1
    past = cache_k.shape[2]
    n_groups = w_router_group.shape[2]
    n_exp = w_router_expert.shape[2]
    per_group = n_exp // n_groups
    assert n_groups + n_exp <= LANES and 2 * conv_dim % 512 == 0 and attn_w % 512 == 0
    t = N_META + seq
    npr = bp * t
    ns = bs * tq
    n_all = npr + ns

    w_in_bf = w_in[0].astype(BF16)
    c4 = 2 * conv_dim + 3 * attn_w
    wf = jnp.pad(w_in[0][:, c4:], ((0, 0), (0, LANES - n_heads))).astype(BF16)
    bf = jnp.pad(b_forget[0], (0, LANES - n_heads)).reshape(1, LANES)
    w_out_bf = w_out[0].astype(BF16)
    w_r = jnp.pad(jnp.concatenate([w_router_group[0], w_router_expert[0]], axis=1),
                  ((0, 0), (0, LANES - n_groups - n_exp))).astype(BF16)
    b_r = jnp.pad(jnp.concatenate([b_router_group[0], b_router_expert[0]]),
                  (0, LANES - n_groups - n_exp)).reshape(1, LANES)

    h_p = jnp.concatenate([jnp.broadcast_to(meta_tokens[None], (bp, N_META, d)), x_prompt], axis=1).reshape(npr, d)
    h_s = x_sample.reshape(ns, d)

    tm_p = _div_tile(npr, 1400, 16)
    tm_s = _div_tile(ns, 1400, 16)
    mix = functools.partial(_mixer_inputs, norm_g=norm_mix_g[0], w_in_bf=w_in_bf, wf=wf, bf=bf, q_g=q_norm_g[0],
                            k_g=k_norm_g[0], conv_dim=conv_dim, attn_w=attn_w, n_heads=n_heads)
    glu_p, q_p, k32_p, k16_p, v32_p, v16_p, logf_p = mix(h_p, tm=tm_p, q_scale=hd ** -0.5 * LOG2E)
    glu_s, q_s, k32_s, k16_s, v32_s, v16_s, logf_s = mix(h_s, tm=tm_s, q_scale=hd ** -0.5)

    glu_p3 = glu_p.reshape(bp, t, conv_dim)
    glu_s3 = glu_s.reshape(bs, tq, conv_dim)
    hist_p = jnp.zeros((bp, hist_len + 2, conv_dim), F32)
    hist_s = jnp.pad(cache_conv[0], ((0, 0), (2, 0), (0, 0)))
    conv_args = (conv_w[0], conv_b[0], conv_ln_g[0], conv_ln_b[0])
    conv_p = _conv_module(glu_p3, hist_p, *conv_args)
    conv_s = _conv_module(glu_s3, hist_s, *conv_args)

    logf_p3 = logf_p.reshape(bp, t, n_heads)
    logf_s3 = logf_s.reshape(bs, tq, n_heads)
    cblk = 256
    pad_t = lambda a: jnp.pad(a, ((0, 0), (0, _cdiv(a.shape[1], cblk) * cblk - a.shape[1]), (0, 0)))
    cum_p = _cumsum_time(pad_t(logf_p3), cblk)[:, :t]
    lf_all = jnp.concatenate([cache_logf[0], logf_s3], axis=1)
    cum_s = _cumsum_time(pad_t(lf_all), cblk)[:, :past + tq]
    r3 = lambda a, b_, t_: a.reshape(b_, t_, attn_w)
    attn_p = _attn_prompt(r3(q_p, bp, t), r3(k16_p, bp, t), r3(v16_p, bp, t), cum_p, n_heads)
    attn_s = _attn_sample(r3(q_s, bs, tq), cache_k[0].reshape(bs, past, attn_w), cache_v[0].reshape(bs, past, attn_w),
                          r3(k16_s, bs, tq), r3(v16_s, bs, tq), cum_s, n_heads)

    tn_o = 512
    h2 = _outproj(attn_p.reshape(npr, attn_w), conv_p.reshape(npr, conv_dim), h_p,
                  attn_s.reshape(ns, attn_w), conv_s.reshape(ns, conv_dim), h_s, w_out_bf, tm_p, tn_o)

    tm_r = _div_tile(n_all, 128, 8)
    eid, gate, xp = _router(h2, n_all, norm_ffn_g[0], w_r, b_r, n_groups, per_group, tm_r)
    tabs = _dispatch_tables(eid[:, :TOP_K], n_exp)
    x_pad = _gather_rows(xp, tabs)
    y_pad = _moe_mlp(x_pad, w_gate[0], w_up[0], w_down[0], tabs)

    tm_cp = _div_tile(seq, 128, 8)
    tiles_b = seq // tm_cp
    i_p = jnp.arange(bp * tiles_b, dtype=jnp.int32)
    row0_p = (i_p // tiles_b) * t + N_META + (i_p % tiles_b) * tm_cp
    tm_cs = _div_tile(ns, 128, 8)
    row0_s = npr + jnp.arange(ns // tm_cs, dtype=jnp.int32) * tm_cs
    y_p = _combine(y_pad, h2, gate, tabs['dest'], row0_p, tm_cp).reshape(bp, seq, d)
    y_s = _combine(y_pad, h2, gate, tabs['dest'], row0_s, tm_cs).reshape(bs, tq, d)

    st = lambda a, b_, t_: a.reshape(1, b_, t_, n_heads, hd)
    return (y_p, y_s,
            st(k32_p, bp, t), st(v32_p, bp, t), logf_p3[None], glu_p3[:, t - hist_len:][None],
            st(k32_s, bs, tq), st(v32_s, bs, tq), logf_s3[None],
            jnp.concatenate([cache_conv[0], glu_s3], axis=1)[:, -hist_len:][None])
```

```python
import functools

import jax
import jax.numpy as jnp
from jax import lax
from jax.experimental import pallas as pl
from jax.experimental.pallas import tpu as pltpu

F32 = jnp.float32
BF16 = jnp.bfloat16
U32 = jnp.uint32

N_META = 16
CONV_GROUPS = 16
TOP_K = 2
EPS = 1e-6
NEG_INF = -1e30

LANES = 128
SUBLANES = 8
ATTN_BLOCK = 512
MOE_SUB = 128
MOE_SUBS_PER_CHUNK = 5
DMA_UNROLL = 8
GATHER_GROUP = 32
LOG2E = 1.4426950408889634
BIAS_PARTS = 3
VMEM_LIMIT = 56 * 1024 * 1024


def _cdiv(a, b):
    return -(-a // b)


def _gcd(a, b):
    while b:
        a, b = b, a % b
    return a


def _div_tile(n, target, mult):
    best = None
    for d in range(mult, min(n, target) + 1, mult):
        if n % d == 0:
            best = d
    assert best is not None, (n, target, mult)
    return best


def _params(*sem):
    return pltpu.CompilerParams(dimension_semantics=sem, vmem_limit_bytes=VMEM_LIMIT)


def _rmsnorm_kernel(x_ref, g_ref, o_ref):
    x = x_ref[...]
    ms = jnp.mean(x * x, axis=-1, keepdims=True)
    o_ref[...] = (x * lax.rsqrt(ms + EPS) * g_ref[...]).astype(o_ref.dtype)


def _rmsnorm(x, g, tm):
    n, d = x.shape
    return pl.pallas_call(
        _rmsnorm_kernel,
        out_shape=jax.ShapeDtypeStruct((n, d), BF16),
        grid=(n // tm,),
        in_specs=[pl.BlockSpec((tm, d), lambda i: (i, 0)), pl.BlockSpec((1, d), lambda i: (0, 0))],
        out_specs=pl.BlockSpec((tm, d), lambda i: (i, 0)),
        compiler_params=_params("parallel"),
        name="rmsnorm",
    )(x, g.reshape(1, d))


def _glu_kernel(x_ref, wa_ref, wb_ref, o_ref):
    x = x_ref[...]
    a = jnp.dot(x, wa_ref[...], preferred_element_type=F32)
    b = jnp.dot(x, wb_ref[...], preferred_element_type=F32)
    o_ref[...] = a * jax.nn.sigmoid(b)


def _proj_glu(xn, w, conv_dim, tm, tn):
    n, d = xn.shape
    nb = conv_dim // tn
    return pl.pallas_call(
        _glu_kernel,
        out_shape=jax.ShapeDtypeStruct((n, conv_dim), F32),
        grid=(n // tm, nb),
        in_specs=[pl.BlockSpec((tm, d), lambda i, j: (i, 0)),
                  pl.BlockSpec((d, tn), lambda i, j: (0, j)),
                  pl.BlockSpec((d, tn), lambda i, j: (0, nb + j))],
        out_specs=pl.BlockSpec((tm, tn), lambda i, j: (i, j)),
        compiler_params=_params("parallel", "arbitrary"),
        name="proj_glu",
    )(xn, w, w)


def _headnorm_kernel(x_ref, w_ref, g_ref, *o_refs, scale, hd):
    z = jnp.dot(x_ref[...], w_ref[...], preferred_element_type=F32)
    g = g_ref[...]
    for h in range(z.shape[1] // hd):
        zh = z[:, h * hd:(h + 1) * hd]
        y = zh * lax.rsqrt(jnp.mean(zh * zh, axis=-1, keepdims=True) + EPS) * g
        if scale is not None:
            y = y * scale
        for o_ref in o_refs:
            o_ref[:, h * hd:(h + 1) * hd] = y.astype(o_ref.dtype)


def _proj_headnorm(xn, w, g, col0, width, hd, scale, out_dtypes, tm, tn):
    n, d = xn.shape
    nb = width // tn
    b0 = col0 // tn
    outs = pl.pallas_call(
        functools.partial(_headnorm_kernel, scale=scale, hd=hd),
        out_shape=[jax.ShapeDtypeStruct((n, width), dt) for dt in out_dtypes],
        grid=(n // tm, nb),
        in_specs=[pl.BlockSpec((tm, d), lambda i, j: (i, 0)),
                  pl.BlockSpec((d, tn), lambda i, j: (0, b0 + j)),
                  pl.BlockSpec((1, hd), lambda i, j: (0, 0))],
        out_specs=[pl.BlockSpec((tm, tn), lambda i, j: (i, j)) for _ in out_dtypes],
        compiler_params=_params("parallel", "arbitrary"),
        name="proj_headnorm",
    )(xn, w, g.reshape(1, hd))
    return outs


def _plain_kernel(x_ref, w_ref, *o_refs):
    z = jnp.dot(x_ref[...], w_ref[...], preferred_element_type=F32)
    for o_ref in o_refs:
        o_ref[...] = z.astype(o_ref.dtype)


def _proj_plain(xn, w, col0, width, out_dtypes, tm, tn):
    n, d = xn.shape
    nb = width // tn
    b0 = col0 // tn
    return pl.pallas_call(
        _plain_kernel,
        out_shape=[jax.ShapeDtypeStruct((n, width), dt) for dt in out_dtypes],
        grid=(n // tm, nb),
        in_specs=[pl.BlockSpec((tm, d), lambda i, j: (i, 0)),
                  pl.BlockSpec((d, tn), lambda i, j: (0, b0 + j))],
        out_specs=[pl.BlockSpec((tm, tn), lambda i, j: (i, j)) for _ in out_dtypes],
        compiler_params=_params("parallel", "arbitrary"),
        name="proj_plain",
    )(xn, w)


def _logf_kernel(x_ref, w_ref, b_ref, o_ref):
    z = jnp.dot(x_ref[...], w_ref[...], preferred_element_type=F32) + b_ref[...]
    o_ref[...] = jnp.minimum(z, 0.0) - jnp.log1p(jnp.exp(-jnp.abs(z)))


def _proj_logf(xn, wf, bf, tm):
    n, d = xn.shape
    return pl.pallas_call(
        _logf_kernel,
        out_shape=jax.ShapeDtypeStruct((n, LANES), F32),
        grid=(n // tm,),
        in_specs=[pl.BlockSpec((tm, d), lambda i: (i, 0)),
                  pl.BlockSpec((d, LANES), lambda i: (0, 0)),
                  pl.BlockSpec((1, LANES), lambda i: (0, 0))],
        out_specs=pl.BlockSpec((tm, LANES), lambda i: (i, 0)),
        compiler_params=_params("parallel"),
        name="proj_logf",
    )(xn, wf, bf)


def _conv_kernel(glu_ref, hist_ref, cw_ref, cb_ref, lg_ref, lb_ref, o_ref, full_sc, *, t, tt, width):
    hp = hist_ref.shape[1]
    lanes = glu_ref.shape[2]
    full_sc[0:hp, :] = hist_ref[0]
    full_sc[hp:hp + t, :] = glu_ref[0]
    full_sc[hp + t:hp + t + SUBLANES, :] = jnp.zeros((SUBLANES, lanes), F32)

    def chunk(ci, carry):
        t0 = pl.multiple_of(ci * tt, 16)
        for lt in range(lanes // LANES):
            ls = slice(lt * LANES, (lt + 1) * LANES)
            win = full_sc[pl.ds(t0, tt + hp + SUBLANES), ls]
            y = None
            for r in range(SUBLANES):
                part = None
                for a in range(_cdiv(width + 2, SUBLANES)):
                    j = SUBLANES * a + r
                    if not 2 <= j < width + 2:
                        continue
                    term = win[SUBLANES * a:SUBLANES * a + tt + SUBLANES, :] * cw_ref[j - 2:j - 1, ls]
                    part = term if part is None else part + term
                if part is None:
                    continue
                shifted = part[r:r + tt, :]
                y = shifted if y is None else y + shifted
            y = y + cb_ref[:, ls]
            mu = jnp.mean(y, axis=-1, keepdims=True)
            dlt = y - mu
            var = jnp.mean(dlt * dlt, axis=-1, keepdims=True)
            yn = dlt * lax.rsqrt(var + EPS) * lg_ref[:, ls] + lb_ref[:, ls]
            o_ref[0, pl.ds(t0, tt), ls] = (yn * jax.nn.sigmoid(yn)).astype(o_ref.dtype)
        return carry

    lax.fori_loop(0, t // tt, chunk, 0)


def _conv_module(glu, hist, conv_w, conv_b, ln_g, ln_b):
    b, t, c = glu.shape
    width = conv_w.shape[0]
    hp = hist.shape[1]
    assert hp == 32 and width + 1 == hp and c // CONV_GROUPS == LANES
    tt = _div_tile(t, 64, 16)
    tc = 8 * LANES
    cw = jnp.concatenate([conv_w, jnp.zeros((hp - width, c), F32)], axis=0)
    vec = lambda v: v.reshape(1, c)
    return pl.pallas_call(
        functools.partial(_conv_kernel, t=t, tt=tt, width=width),
        out_shape=jax.ShapeDtypeStruct((b, t, c), BF16),
        grid=(b, c // tc),
        in_specs=[pl.BlockSpec((1, t, tc), lambda i, j: (i, 0, j)),
                  pl.BlockSpec((1, hp, tc), lambda i, j: (i, 0, j)),
                  pl.BlockSpec((hp, tc), lambda i, j: (0, j)),
                  pl.BlockSpec((1, tc), lambda i, j: (0, j)),
                  pl.BlockSpec((1, tc), lambda i, j: (0, j)),
                  pl.BlockSpec((1, tc), lambda i, j: (0, j))],
        out_specs=pl.BlockSpec((1, t, tc), lambda i, j: (i, 0, j)),
        scratch_shapes=[pltpu.VMEM((hp + t + SUBLANES, tc), F32)],
        compiler_params=_params("parallel", "parallel"),
        name="conv_ln_swish",
    )(glu, hist, cw, vec(conv_b), vec(ln_g), vec(ln_b))


def _cumsum_kernel(x_ref, o_ref, *, nblk, blk):
    r = lax.broadcasted_iota(jnp.int32, (blk, blk), 0)
    c = lax.broadcasted_iota(jnp.int32, (blk, blk), 1)
    tri = (c <= r).astype(F32)
    carry = jnp.zeros((1, x_ref.shape[2]), F32)
    for i in range(nblk):
        x = x_ref[0, i * blk:(i + 1) * blk, :]
        o_ref[0, i * blk:(i + 1) * blk, :] = jnp.dot(
            tri, x, precision=lax.Precision.HIGHEST, preferred_element_type=F32) + carry
        carry = carry + jnp.sum(x, axis=0, keepdims=True)


def _cumsum_time(x, blk):
    b, t, h = x.shape
    return pl.pallas_call(
        functools.partial(_cumsum_kernel, nblk=t // blk, blk=blk),
        out_shape=jax.ShapeDtypeStruct((b, t, h), F32),
        grid=(b,),
        in_specs=[pl.BlockSpec((1, t, h), lambda i: (i, 0, 0))],
        out_specs=pl.BlockSpec((1, t, h), lambda i: (i, 0, 0)),
        compiler_params=_params("parallel"),
        name="logf_cumsum",
    )(x)


def _attn_update(q, k, v, cq, ck, mask, m, l, acc):
    s = lax.dot_general(q, k, (((1,), (1,)), ((), ())), preferred_element_type=F32)
    s = s - ck
    if mask is not None:
        s = jnp.where(mask, s, NEG_INF)
    m_new = jnp.maximum(m, jnp.max(s, axis=-1, keepdims=True) + cq)
    alpha = jnp.exp(m - m_new)
    p = jnp.exp(s + (cq - m_new))
    l_new = alpha * l + jnp.sum(p, axis=-1, keepdims=True)
    acc_new = alpha * acc + jnp.dot(p.astype(BF16), v, preferred_element_type=F32)
    return m_new, l_new, acc_new


def _softmax_init(tq, d):
    return (jnp.full((tq, 1), NEG_INF, F32), jnp.zeros((tq, 1), F32), jnp.zeros((tq, d), F32))


def _head_column(c, h):
    sel = lax.broadcasted_iota(jnp.int32, c.shape, 1) == h
    return jnp.sum(jnp.where(sel, c, 0.0), axis=-1, keepdims=True)


def _causal_mask(n):
    r = lax.broadcasted_iota(jnp.int32, (n, n), 0)
    c = lax.broadcasted_iota(jnp.int32, (n, n), 1)
    return c <= r


def _attn_prompt_kernel(q_ref, k_ref, v_ref, cum_ref, o_ref, qa_sc, ka_sc, s_sc, d_sc, p_sc, pd_sc, l_sc,
                        *, t, bq, rs):
    h = pl.program_id(1)
    d = q_ref.shape[2]
    nfull = t // bq
    tail = t - nfull * bq
    nt = (((1,), (1,)), ((), ()))

    lane = lax.broadcasted_iota(jnp.int32, (t, d), 1)
    rem = _head_column(cum_ref[0], h) * LOG2E
    aug = jnp.zeros((t, d), F32)
    for j in range(BIAS_PARTS):
        part = rem.astype(BF16).astype(F32)
        aug = jnp.where(lane == j, part, aug)
        rem = rem - part
    ka_sc[:, 0:d] = k_ref[0]
    ka_sc[:, d:2 * d] = aug.astype(BF16)
    qa_sc[:, 0:d] = q_ref[0]
    qa_sc[:, d:2 * d] = jnp.where(lane < BIAS_PARTS, -1.0, 0.0).astype(BF16)

    blocks = [(i * bq, bq) for i in range(nfull)] + ([(nfull * bq, tail)] if tail else [])
    for q0, nq in blocks:
        q = qa_sc[q0:q0 + nq, :]
        if q0:
            s_sc[0:nq, 0:q0] = lax.dot_general(q, ka_sc[0:q0, :], nt, preferred_element_type=F32)
        d_sc[0:nq, 0:nq] = lax.dot_general(q, ka_sc[q0:q0 + nq, :], nt, preferred_element_type=F32)
        rows = min(rs, nq)

        def softmax_rows(ri, carry, q0=q0, nq=nq, rows=rows):
            r0 = ri * rows
            cq = _head_column(cum_ref[0, pl.ds(q0 + r0, rows), :], h) * LOG2E
            rr = lax.broadcasted_iota(jnp.int32, (rows, nq), 0) + r0
            cc = lax.broadcasted_iota(jnp.int32, (rows, nq), 1)
            sd = jnp.where(cc <= rr, d_sc[pl.ds(r0, rows), 0:nq], NEG_INF)
            mx = jnp.max(sd, axis=-1, keepdims=True)
            if q0:
                sp = s_sc[pl.ds(r0, rows), 0:q0]
                mx = jnp.maximum(mx, jnp.max(sp, axis=-1, keepdims=True))
            shift = cq - (mx + cq)
            pd = jnp.exp2(sd + shift)
            l = jnp.sum(pd, axis=-1, keepdims=True)
            pd_sc[pl.ds(r0, rows), 0:nq] = pd.astype(BF16)
            if q0:
                pp = jnp.exp2(sp + shift)
                l = l + jnp.sum(pp, axis=-1, keepdims=True)
                p_sc[pl.ds(r0, rows), 0:q0] = pp.astype(BF16)
            l_sc[pl.ds(r0, rows), :] = l
            return carry

        for ri in range(nq // rows):
            softmax_rows(ri, 0)
        o = jnp.dot(pd_sc[0:nq, 0:nq], v_ref[0, q0:q0 + nq, :], preferred_element_type=F32)
        if q0:
            o = o + jnp.dot(p_sc[0:nq, 0:q0], v_ref[0, 0:q0, :], preferred_element_type=F32)
        o_ref[0, q0:q0 + nq, :] = (o / l_sc[0:nq, :]).astype(o_ref.dtype)


def _blocked_rows(cum, blk):
    b, t, h = cum.shape
    nblk = _cdiv(t, blk)
    c = jnp.pad(cum, ((0, 0), (0, nblk * blk - t), (0, 0)))
    return c.transpose(0, 2, 1).reshape(b, h, nblk, 1, blk)


def _attn_prompt(q, k, v, cum, n_heads):
    b, t, w = q.shape
    d = w // n_heads
    bq = ATTN_BLOCK
    nfull = t // bq
    assert nfull >= 1 and (t % bq) % 16 == 0
    blk = lambda: pl.BlockSpec((1, t, d), lambda i, j: (i, 0, j))
    return pl.pallas_call(
        functools.partial(_attn_prompt_kernel, t=t, bq=bq, rs=128),
        out_shape=jax.ShapeDtypeStruct((b, t, w), BF16),
        grid=(b, n_heads),
        in_specs=[blk(), blk(), blk(),
                  pl.BlockSpec((1, t, n_heads), lambda i, j: (i, 0, 0))],
        out_specs=blk(),
        scratch_shapes=[pltpu.VMEM((t, 2 * d), BF16), pltpu.VMEM((t, 2 * d), BF16),
                        pltpu.VMEM((bq, nfull * bq), F32), pltpu.VMEM((bq, bq), F32),
                        pltpu.VMEM((bq, nfull * bq), BF16), pltpu.VMEM((bq, bq), BF16),
                        pltpu.VMEM((bq, 1), F32)],
        compiler_params=_params("parallel", "parallel"),
        name="fox_attn_prompt",
    )(q, k, v, cum)


def _attn_sample_kernel(q_ref, ck_ref, cv_ref, k_ref, v_ref, cum_ref, cumt_ref, o_ref, *, past, bk, hpb, d):
    tq = q_ref.shape[1]
    for hh in range(hpb):
        cols = slice(hh * d, (hh + 1) * d)
        q = q_ref[0, :, cols]
        cq = _head_column(cum_ref[0], pl.program_id(1) * hpb + hh)
        carry = _softmax_init(tq, d)
        for kj in range(past // bk):
            kb = ck_ref[0, kj * bk:(kj + 1) * bk, cols].astype(BF16)
            vb = cv_ref[0, kj * bk:(kj + 1) * bk, cols].astype(BF16)
            carry = _attn_update(q, kb, vb, cq, cumt_ref[0, hh, kj], None, *carry)
        carry = _attn_update(q, k_ref[0, :, cols], v_ref[0, :, cols], cq, cumt_ref[0, hh, past // bk][:, :tq],
                             _causal_mask(tq), *carry)
        _, l, acc = carry
        o_ref[0, :, cols] = (acc / l).astype(o_ref.dtype)


def _attn_sample(q, cache_k, cache_v, k, v, cum, n_heads):
    b, tq, w = q.shape
    d = w // n_heads
    past = cache_k.shape[1]
    bk = _div_tile(past, ATTN_BLOCK, LANES)
    assert tq <= bk
    cumt = _blocked_rows(cum, bk)
    nblk = cumt.shape[2]
    cum_q = cum[:, past:]
    hpb = 4 if n_heads % 4 == 0 else 1
    new = lambda: pl.BlockSpec((1, tq, hpb * d), lambda i, j: (i, 0, j))
    old = lambda: pl.BlockSpec((1, past, hpb * d), lambda i, j: (i, 0, j))
    return pl.pallas_call(
        functools.partial(_attn_sample_kernel, past=past, bk=bk, hpb=hpb, d=d),
        out_shape=jax.ShapeDtypeStruct((b, tq, w), BF16),
        grid=(b, n_heads // hpb),
        in_specs=[new(), old(), old(), new(), new(),
                  pl.BlockSpec((1, tq, n_heads), lambda i, j: (i, 0, 0)),
                  pl.BlockSpec((1, hpb, nblk, 1, bk), lambda i, j: (i, j, 0, 0, 0))],
        out_specs=new(),
        compiler_params=_params("parallel", "parallel"),
        name="fox_attn_sample",
    )(q, cache_k, cache_v, k, v, cum_q, cumt)


def _outproj_kernel(ap_ref, cp_ref, hp_ref, as_ref, cs_ref, hs_ref, wa_ref, wc_ref, o_ref, *, n_p):
    i = pl.program_id(0)

    def project(a_ref, c_ref, h_ref):
        return (h_ref[...]
                + jnp.dot(a_ref[...], wa_ref[...], preferred_element_type=F32)
                + jnp.dot(c_ref[...], wc_ref[...], preferred_element_type=F32))

    @pl.when(i < n_p)
    def _():
        o_ref[...] = project(ap_ref, cp_ref, hp_ref)

    @pl.when(i == n_p)
    def _():
        o_ref[0:hs_ref.shape[0], :] = project(as_ref, cs_ref, hs_ref)


def _outproj(attn_p, conv_p, h_p, attn_s, conv_s, h_s, w_out, tm, tn):
    npr, wa = attn_p.shape
    ns = attn_s.shape[0]
    wc = conv_p.shape[1]
    d = w_out.shape[1]
    assert wa == wc and npr % tm == 0 and ns <= tm
    n_p = npr // tm
    last = lambda i: jnp.minimum(i, n_p - 1)
    return pl.pallas_call(
        functools.partial(_outproj_kernel, n_p=n_p),
        out_shape=jax.ShapeDtypeStruct(((n_p + 1) * tm, d), F32),
        grid=(n_p + 1, d // tn),
        in_specs=[pl.BlockSpec((tm, wa), lambda i, j: (last(i), 0)),
                  pl.BlockSpec((tm, wc), lambda i, j: (last(i), 0)),
                  pl.BlockSpec((tm, tn), lambda i, j: (last(i), j)),
                  pl.BlockSpec((ns, wa), lambda i, j: (0, 0)),
                  pl.BlockSpec((ns, wc), lambda i, j: (0, 0)),
                  pl.BlockSpec((ns, tn), lambda i, j: (0, j)),
                  pl.BlockSpec((wa, tn), lambda i, j: (0, j)),
                  pl.BlockSpec((wc, tn), lambda i, j: (1, j))],
        out_specs=pl.BlockSpec((tm, tn), lambda i, j: (i, j)),
        compiler_params=_params("parallel", "arbitrary"),
        name="outproj_residual",
    )(attn_p, conv_p, h_p, attn_s, conv_s, h_s, w_out, w_out)


def _router_kernel(h_ref, g_ref, w_ref, b_ref, eid_ref, gate_ref, xp_ref, *, n_groups, per_group):
    x = h_ref[...]
    xn = (x * lax.rsqrt(jnp.mean(x * x, axis=-1, keepdims=True) + EPS) * g_ref[...]).astype(BF16)
    half = x.shape[1] // 2
    bits = pltpu.bitcast(xn.astype(F32), U32)
    xp_ref[...] = bits[:, :half] | (bits[:, half:] >> 16)

    lg = jnp.dot(xn, w_ref[...], preferred_element_type=F32) + b_ref[...]
    lane = lax.broadcasted_iota(jnp.int32, lg.shape, 1)
    ninf = -jnp.inf

    def top1(vals):
        mx = jnp.max(vals, axis=-1, keepdims=True)
        idx = jnp.min(jnp.where(vals == mx, lane, LANES), axis=-1, keepdims=True)
        return mx, idx

    gl = jnp.where(lane < n_groups, lg, ninf)
    gmax, gidx = top1(gl)
    p_g = 1.0 / jnp.sum(jnp.exp(gl - gmax), axis=-1, keepdims=True)
    lo = n_groups + per_group * gidx
    el = jnp.where((lane >= lo) & (lane < lo + per_group), lg, ninf)
    m1, i1 = top1(el)
    m2, i2 = top1(jnp.where(lane == i1, ninf, el))
    e21 = jnp.exp(m2 - m1)
    g1 = p_g / (1.0 + e21)
    g2 = g1 * e21
    eid_ref[...] = jnp.where(lane == 0, i1 - n_groups, jnp.where(lane == 1, i2 - n_groups, 0))
    gate_ref[...] = jnp.where(lane == 0, g1, jnp.where(lane == 1, g2, 0.0))


def _router(h, n, g, w_r, b_r, n_groups, per_group, tm):
    d = h.shape[1]
    row = lambda w: pl.BlockSpec((tm, w), lambda i: (i, 0))
    return pl.pallas_call(
        functools.partial(_router_kernel, n_groups=n_groups, per_group=per_group),
        out_shape=[jax.ShapeDtypeStruct((n, LANES), jnp.int32), jax.ShapeDtypeStruct((n, LANES), F32),
                   jax.ShapeDtypeStruct((n, d // 2), U32)],
        grid=(n // tm,),
        in_specs=[row(d),
                  pl.BlockSpec((1, d), lambda i: (0, 0)),
                  pl.BlockSpec((d, LANES), lambda i: (0, 0)),
                  pl.BlockSpec((1, LANES), lambda i: (0, 0))],
        out_specs=[row(LANES), row(LANES), row(d // 2)],
        compiler_params=_params("parallel"),
        name="router",
    )(h, g.reshape(1, d), w_r, b_r)


def _dispatch_tables(e_id, n_exp):
    n, k = e_id.shape
    m = n * k
    sb = MOE_SUB
    rc = sb * MOE_SUBS_PER_CHUNK
    n_chunks = _cdiv(m, rc) + n_exp
    n_sub_max = _cdiv(m, sb) + n_exp
    e_flat = e_id.reshape(m)
    onehot = (e_flat[:, None] == jnp.arange(n_exp, dtype=jnp.int32)[None, :]).astype(jnp.int32)
    csum = jnp.cumsum(onehot, axis=0)
    counts = csum[-1]
    rank = jnp.take_along_axis(csum, e_flat[:, None], axis=1)[:, 0] - 1
    cpe = (counts + rc - 1) // rc
    chunk_end = jnp.cumsum(cpe)
    chunk_start = chunk_end - cpe
    n_used = chunk_end[-1]
    dest = (chunk_start[e_flat] * rc + rank).astype(jnp.int32)
    c_ar = jnp.arange(n_chunks, dtype=jnp.int32)
    used = c_ar < n_used
    c_cl = jnp.minimum(c_ar, n_used - 1)
    chunk_e = jnp.minimum(jnp.searchsorted(chunk_end, c_cl, side='right'), n_exp - 1).astype(jnp.int32)
    rows = jnp.clip(counts[chunk_e] - (c_cl - chunk_start[chunk_e]) * rc, 0, rc)
    nsub = jnp.where(used, (rows + sb - 1) // sb, 0).astype(jnp.int32)
    in_blk = c_cl.astype(jnp.int32)
    out_blk = jnp.where(used, c_ar, n_chunks).astype(jnp.int32)
    tok = (jnp.arange(m, dtype=jnp.int32) // k)
    src = jnp.zeros((n_chunks * rc,), jnp.int32).at[dest].set(tok)
    sub_active = (jnp.arange(MOE_SUBS_PER_CHUNK, dtype=jnp.int32)[None, :] < nsub[:, None]).reshape(-1)
    n_act = jnp.sum(sub_active.astype(jnp.int32))
    sub_ids = jnp.nonzero(sub_active, size=n_sub_max, fill_value=0)[0].astype(jnp.int32)
    sub_ids = jnp.where(jnp.arange(n_sub_max) < n_act, sub_ids, sub_ids[n_act - 1])
    src_sub = src.reshape(n_chunks * MOE_SUBS_PER_CHUNK, sb)[sub_ids].reshape(-1)
    rows_sub = jnp.clip(rows[:, None] - jnp.arange(MOE_SUBS_PER_CHUNK, dtype=jnp.int32)[None, :] * sb, 0, sb)
    grp_sub = ((rows_sub.reshape(-1)[sub_ids] + GATHER_GROUP - 1) // GATHER_GROUP).astype(jnp.int32)
    return dict(dest=dest, chunk_e=chunk_e, nsub=nsub, in_blk=in_blk, out_blk=out_blk,
                sub_ids=sub_ids, src_sub=src_sub, grp_sub=grp_sub, n_act=n_act.reshape(1).astype(jnp.int32),
                n_chunks=n_chunks, n_sub_max=n_sub_max)


def _gather_kernel(src_ref, sub_ref, nact_ref, grp_ref, xp_hbm, o_ref, buf, sem, *, sb):
    i = pl.program_id(0)
    n_act = nact_ref[0]
    half = xp_hbm.shape[1]

    def issue(step, slot):
        def body(it, carry):
            for u in range(DMA_UNROLL):
                r = it * DMA_UNROLL + u
                tok = src_ref[step * sb + r]
                pltpu.make_async_copy(xp_hbm.at[pl.ds(tok, 1)], buf.at[slot, pl.ds(r, 1)], sem.at[slot]).start()
            return carry
        lax.fori_loop(0, grp_ref[step] * (GATHER_GROUP // DMA_UNROLL), body, 0)

    @pl.when(i == 0)
    def _():
        buf[...] = jnp.zeros(buf.shape, buf.dtype)
        issue(0, 0)

    @pl.when(i + 1 < n_act)
    def _():
        issue(i + 1, (i + 1) % 2)

    @pl.when(i < n_act)
    def _():
        slot = i % 2

        def wait_group(gi, carry):
            pltpu.make_async_copy(xp_hbm.at[pl.ds(0, GATHER_GROUP)], buf.at[slot, pl.ds(0, GATHER_GROUP)],
                                  sem.at[slot]).wait()
            return carry
        lax.fori_loop(0, grp_ref[i], wait_group, 0)
        rows = 32
        for c in range(sb // rows):
            u = buf[slot, c * rows:(c + 1) * rows, :]
            hi = pltpu.bitcast(u & jnp.uint32(0xFFFF0000), F32)
            lo = pltpu.bitcast(u << 16, F32)
            o_ref[c * rows:(c + 1) * rows, 0:half] = hi.astype(o_ref.dtype)
            o_ref[c * rows:(c + 1) * rows, half:2 * half] = lo.astype(o_ref.dtype)


def _gather_rows(xp, tabs):
    n, half = xp.shape
    sb = MOE_SUB
    rows = tabs['n_chunks'] * sb * MOE_SUBS_PER_CHUNK
    return pl.pallas_call(
        functools.partial(_gather_kernel, sb=sb),
        out_shape=jax.ShapeDtypeStruct((rows, 2 * half), BF16),
        grid_spec=pltpu.PrefetchScalarGridSpec(
            num_scalar_prefetch=4,
            grid=(tabs['n_sub_max'],),
            in_specs=[pl.BlockSpec(memory_space=pl.ANY)],
            out_specs=pl.BlockSpec((sb, 2 * half), lambda i, src, sub, nact, grp: (sub[i], 0)),
            scratch_shapes=[pltpu.VMEM((2, sb, half), U32), pltpu.SemaphoreType.DMA((2,))]),
        compiler_params=_params("arbitrary"),
        name="moe_gather",
    )(tabs['src_sub'], tabs['sub_ids'], tabs['n_act'], tabs['grp_sub'], xp)


def _moe_kernel(cea_ref, nsa_ref, iba_ref, ceb_ref, nsb_ref, obb_ref, x_hbm, wg_ref, wu_ref, wd_ref, o_ref,
                hid_sc, x_sc, x_sem, *, sb, n_t, tf):
    g = pl.program_id(0)
    k = pl.program_id(1)
    nsa = nsa_ref[g]
    nsb = nsb_ref[g]
    cur = g % 2
    rc = x_sc.shape[1]

    def x_copy(chunk, slot):
        row0 = pl.multiple_of(iba_ref[chunk] * rc, rc)
        return pltpu.make_async_copy(x_hbm.at[pl.ds(row0, rc)], x_sc.at[slot], x_sem.at[slot])

    @pl.when(k == 0)
    def _():
        @pl.when((g == 0) & (nsa > 0))
        def _():
            x_copy(0, 0).start()

        nxt = jnp.minimum(g + 1, pl.num_programs(0) - 1)

        @pl.when((g + 1 < pl.num_programs(0)) & (nsa_ref[nxt] > 0))
        def _():
            x_copy(nxt, 1 - cur).start()

        @pl.when(nsa > 0)
        def _():
            x_copy(g, cur).wait()

    for r in range(1, MOE_SUBS_PER_CHUNK + 1):
        rows = r * sb

        @pl.when(nsb == r)
        def _():
            o_ref[0:rows, :] = jnp.dot(hid_sc[1 - cur, 0:rows, :], wd_ref[0].astype(BF16),
                                       preferred_element_type=F32)

        @pl.when(nsa == r)
        def _():
            x = x_sc[cur, 0:rows, :]
            a = jnp.dot(x, wg_ref[0].astype(BF16), preferred_element_type=F32)
            u = jnp.dot(x, wu_ref[0].astype(BF16), preferred_element_type=F32)
            hid = (a * jax.nn.sigmoid(a) * u).astype(BF16)
            for f in range(n_t):
                @pl.when(k == f)
                def _():
                    hid_sc[cur, 0:rows, f * tf:(f + 1) * tf] = hid


def _moe_mlp(x_pad, w_gate, w_up, w_down, tabs):
    n_exp, d, ff = w_gate.shape
    sb = MOE_SUB
    rc = sb * MOE_SUBS_PER_CHUNK
    n_t = 4
    tf = ff // n_t
    td = d // n_t
    n_chunks = tabs['n_chunks']
    i32 = lambda v: jnp.asarray(v, jnp.int32).reshape(1)
    ce, ns, ib, ob = tabs['chunk_e'], tabs['nsub'], tabs['in_blk'], tabs['out_blk']
    ce_a = jnp.concatenate([ce, ce[-1:]])
    ns_a = jnp.concatenate([ns, i32(0)])
    ib_a = jnp.concatenate([ib, ib[-1:]])
    ce_b = jnp.concatenate([ce[:1], ce])
    ns_b = jnp.concatenate([i32(0), ns])
    ob_b = jnp.concatenate([i32(n_chunks + 1), ob])
    tile = lambda n, g, k: jnp.where(n[g] > 0, k, n_t - 1)
    wmap = lambda g, k, cea, nsa, iba, ceb, nsb, obb: (cea[g], 0, tile(nsa, g, k))
    return pl.pallas_call(
        functools.partial(_moe_kernel, sb=sb, n_t=n_t, tf=tf),
        out_shape=jax.ShapeDtypeStruct(((n_chunks + 2) * rc, d), F32),
        grid_spec=pltpu.PrefetchScalarGridSpec(
            num_scalar_prefetch=6,
            grid=(n_chunks + 1, n_t),
            in_specs=[pl.BlockSpec(memory_space=pl.ANY),
                      pl.BlockSpec((1, d, tf), wmap),
                      pl.BlockSpec((1, d, tf), wmap),
                      pl.BlockSpec((1, ff, td), lambda g, k, cea, nsa, iba, ceb, nsb, obb:
                                   (ceb[g], 0, tile(nsb, g, k)))],
            out_specs=pl.BlockSpec((rc, td), lambda g, k, cea, nsa, iba, ceb, nsb, obb:
                                   (obb[g], jnp.where(nsb[g] > 0, k, 0))),
            scratch_shapes=[pltpu.VMEM((2, rc, ff), BF16), pltpu.VMEM((2, rc, d), BF16),
                            pltpu.SemaphoreType.DMA((2,))]),
        compiler_params=_params("arbitrary", "arbitrary"),
        name="moe_mlp",
    )(ce_a, ns_a, ib_a, ce_b, ns_b, ob_b, x_pad, w_gate, w_up, w_down)


def _combine_kernel(dest_ref, row0_ref, y_hbm, h_hbm, gate_hbm, o_ref, ybuf, hbuf, gbuf, sem, *, tm):
    i = pl.program_id(0)
    n = pl.num_programs(0)

    def contiguous(step, slot):
        r0 = pl.multiple_of(row0_ref[step], SUBLANES)
        return (pltpu.make_async_copy(h_hbm.at[pl.ds(r0, tm)], hbuf.at[slot], sem.at[slot]),
                pltpu.make_async_copy(gate_hbm.at[pl.ds(r0, tm)], gbuf.at[slot], sem.at[slot]))

    def issue(step, slot):
        for cp in contiguous(step, slot):
            cp.start()
        base = row0_ref[step] * TOP_K

        def body(it, carry):
            for u in range(DMA_UNROLL // TOP_K):
                r = it * (DMA_UNROLL // TOP_K) + u
                for k in range(TOP_K):
                    row = dest_ref[base + r * TOP_K + k]
                    pltpu.make_async_copy(y_hbm.at[pl.ds(row, 1)], ybuf.at[slot, k, pl.ds(r, 1)],
                                          sem.at[slot]).start()
            return carry
        lax.fori_loop(0, tm * TOP_K // DMA_UNROLL, body, 0)

    @pl.when(i == 0)
    def _():
        issue(0, 0)

    @pl.when(i + 1 < n)
    def _():
        issue(i + 1, (i + 1) % 2)

    slot = i % 2
    for cp in contiguous(i, slot):
        cp.wait()
    for k in range(TOP_K):
        pltpu.make_async_copy(y_hbm.at[pl.ds(0, tm)], ybuf.at[slot, k], sem.at[slot]).wait()
    gate = gbuf[slot]
    out = hbuf[slot]
    for k in range(TOP_K):
        out = out + gate[:, k:k + 1] * ybuf[slot, k]
    o_ref[...] = out


def _combine(y_pad, h, gate, dest, row0, tm):
    d = h.shape[1]
    n_tiles = row0.shape[0]
    return pl.pallas_call(
        functools.partial(_combine_kernel, tm=tm),
        out_shape=jax.ShapeDtypeStruct((n_tiles * tm, d), F32),
        grid_spec=pltpu.PrefetchScalarGridSpec(
            num_scalar_prefetch=2,
            grid=(n_tiles,),
            in_specs=[pl.BlockSpec(memory_space=pl.ANY), pl.BlockSpec(memory_space=pl.ANY),
                      pl.BlockSpec(memory_space=pl.ANY)],
            out_specs=pl.BlockSpec((tm, d), lambda i, dest, row0: (i, 0)),
            scratch_shapes=[pltpu.VMEM((2, TOP_K, tm, d), F32), pltpu.VMEM((2, tm, d), F32),
                            pltpu.VMEM((2, tm, LANES), F32), pltpu.SemaphoreType.DMA((2,))]),
        compiler_params=_params("arbitrary"),
        name="moe_combine",
    )(dest, row0, y_pad, h, gate)


def _mixer_inputs(h, norm_g, w_in_bf, wf, bf, q_g, k_g, conv_dim, attn_w, n_heads, tm, q_scale):
    hd = attn_w // n_heads
    tn = 512
    xn = _rmsnorm(h, norm_g, _div_tile(h.shape[0], 256, 8))
    glu = _proj_glu(xn, w_in_bf, conv_dim, tm, tn // 2)
    c1 = 2 * conv_dim
    (q,) = _proj_headnorm(xn, w_in_bf, q_g, c1, attn_w, hd, q_scale, (BF16,), tm, tn)
    k32, k16 = _proj_headnorm(xn, w_in_bf, k_g, c1 + attn_w, attn_w, hd, None, (F32, BF16), tm, tn)
    v32, v16 = _proj_plain(xn, w_in_bf, c1 + 2 * attn_w, attn_w, (F32, BF16), tm, tn)
    logf = _proj_logf(xn, wf, bf, tm)[:, :n_heads]
    return glu, q, k32, k16, v32, v16, logf


def kernel(x_prompt, x_sample, cache_k, cache_v, cache_logf, cache_conv, meta_tokens, norm_mix_g, w_in, b_forget,
           q_norm_g, k_norm_g, conv_w, conv_b, conv_ln_g, conv_ln_b, w_out, norm_ffn_g, w_router_group,
           b_router_group, w_router_expert, b_router_expert, w_gate, w_up, w_down):
    depth = w_in.shape[0]
    assert depth == 1
    bp, seq, d = x_prompt.shape
    bs, tq, _ = x_sample.shape
    n_heads = cache_k.shape[3]
    hd = cache_k.shape[4]
    attn_w = n_heads * hd
    conv_dim = conv_w.shape[2]
    hist_len = conv_w.shape[1] - 1
    past = cache_k.shape[2]
    n_groups = w_router_group.shape[2]
    n_exp = w_router_expert.shape[2]
    per_group = n_exp // n_groups
    assert n_groups + n_exp <= LANES and 2 * conv_dim % 512 == 0 and attn_w % 512 == 0
    t = N_META + seq
    npr = bp * t
    ns = bs * tq
    n_all = npr + ns

    w_in_bf = w_in[0].astype(BF16)
    c4 = 2 * conv_dim + 3 * attn_w
    wf = jnp.pad(w_in[0][:, c4:], ((0, 0), (0, LANES - n_heads))).astype(BF16)
    bf = jnp.pad(b_forget[0], (0, LANES - n_heads)).reshape(1, LANES)
    w_out_bf = w_out[0].astype(BF16)
    w_r = jnp.pad(jnp.concatenate([w_router_group[0], w_router_expert[0]], axis=1),
                  ((0, 0), (0, LANES - n_groups - n_exp))).astype(BF16)
    b_r = jnp.pad(jnp.concatenate([b_router_group[0], b_router_expert[0]]),
                  (0, LANES - n_groups - n_exp)).reshape(1, LANES)

    h_p = jnp.concatenate([jnp.broadcast_to(meta_tokens[None], (bp, N_META, d)), x_prompt], axis=1).reshape(npr, d)
    h_s = x_sample.reshape(ns, d)

    tm_p = _div_tile(npr, 1400, 16)
    tm_s = _div_tile(ns, 1400, 16)
    mix = functools.partial(_mixer_inputs, norm_g=norm_mix_g[0], w_in_bf=w_in_bf, wf=wf, bf=bf, q_g=q_norm_g[0],
                            k_g=k_norm_g[0], conv_dim=conv_dim, attn_w=attn_w, n_heads=n_heads)
    glu_p, q_p, k32_p, k16_p, v32_p, v16_p, logf_p = mix(h_p, tm=tm_p, q_scale=hd ** -0.5 * LOG2E)
    glu_s, q_s, k32_s, k16_s, v32_s, v16_s, logf_s = mix(h_s, tm=tm_s, q_scale=hd ** -0.5)

    glu_p3 = glu_p.reshape(bp, t, conv_dim)
    glu_s3 = glu_s.reshape(bs, tq, conv_dim)
    hist_p = jnp.zeros((bp, hist_len + 2, conv_dim), F32)
    hist_s = jnp.pad(cache_conv[0], ((0, 0), (2, 0), (0, 0)))
    conv_args = (conv_w[0], conv_b[0], conv_ln_g[0], conv_ln_b[0])
    conv_p = _conv_module(glu_p3, hist_p, *conv_args)
    conv_s = _conv_module(glu_s3, hist_s, *conv_args)

    logf_p3 = logf_p.reshape(bp, t, n_heads)
    logf_s3 = logf_s.reshape(bs, tq, n_heads)
    cblk = 256
    pad_t = lambda a: jnp.pad(a, ((0, 0), (0, _cdiv(a.shape[1], cblk) * cblk - a.shape[1]), (0, 0)))
    cum_p = _cumsum_time(pad_t(logf_p3), cblk)[:, :t]
    lf_all = jnp.concatenate([cache_logf[0], logf_s3], axis=1)
    cum_s = _cumsum_time(pad_t(lf_all), cblk)[:, :past + tq]
    r3 = lambda a, b_, t_: a.reshape(b_, t_, attn_w)
    attn_p = _attn_prompt(r3(q_p, bp, t), r3(k16_p, bp, t), r3(v16_p, bp, t), cum_p, n_heads)
    attn_s = _attn_sample(r3(q_s, bs, tq), cache_k[0].reshape(bs, past, attn_w), cache_v[0].reshape(bs, past, attn_w),
                          r3(k16_s, bs, tq), r3(v16_s, bs, tq), cum_s, n_heads)

    tn_o = 512
    h2 = _outproj(attn_p.reshape(npr, attn_w), conv_p.reshape(npr, conv_dim), h_p,
                  attn_s.reshape(ns, attn_w), conv_s.reshape(ns, conv_dim), h_s, w_out_bf, tm_p, tn_o)

    tm_r = _div_tile(n_all, 128, 8)
    eid, gate, xp = _router(h2, n_all, norm_ffn_g[0], w_r, b_r, n_groups, per_group, tm_r)
    tabs = _dispatch_tables(eid[:, :TOP_K], n_exp)
    x_pad = _gather_rows(xp, tabs)
    y_pad = _moe_mlp(x_pad, w_gate[0], w_up[0], w_down[0], tabs)

    tm_cp = _div_tile(seq, 128, 8)
    tiles_b = seq // tm_cp
    i_p = jnp.arange(bp * tiles_b, dtype=jnp.int32)
    row0_p = (i_p // tiles_b) * t + N_META + (i_p % tiles_b) * tm_cp
    tm_cs = _div_tile(ns, 128, 8)
    row0_s = npr + jnp.arange(ns // tm_cs, dtype=jnp.int32) * tm_cs
    y_p = _combine(y_pad, h2, gate, tabs['dest'], row0_p, tm_cp).reshape(bp, seq, d)
    y_s = _combine(y_pad, h2, gate, tabs['dest'], row0_s, tm_cs).reshape(bs, tq, d)

    st = lambda a, b_, t_: a.reshape(1, b_, t_, n_heads, hd)
    return (y_p, y_s,
            st(k32_p, bp, t), st(v32_p, bp, t), logf_p3[None], glu_p3[:, t - hist_len:][None],
            st(k32_s, bs, tq), st(v32_s, bs, tq), logf_s3[None],
            jnp.concatenate([cache_conv[0], glu_s3], axis=1)[:, -hist_len:][None])
```
